```python
import jax
import jax.numpy as jnp
from jax import lax
import numpy as np

D_MODEL = 4096
BATCH = 4
SEQ = 2048
DEPTH = 2
DEC_BATCH = 8
DEC_SEQ = 8
PAST_LEN = 16384
PAGE_SIZE = 128

N_MIXERS = 2
N_A_LAYERS = (DEPTH + 1) // 2
N_B_LAYERS = DEPTH // 2
RMS_EPS = 1e-6

N_MEM = 256
MEM_HEADS = 4
MEM_HD = D_MODEL // 16
MEM_W = MEM_HEADS * MEM_HD
MIX_W = D_MODEL - MEM_W

A_HD = 64
A_HEADS = MIX_W // A_HD
DECAY_LORA = 96
AAA_LORA = 96
GATE_LORA = 384
A_SHIFT_W = 3 * MIX_W + DECAY_LORA + AAA_LORA + GATE_LORA
A_IN_W = A_SHIFT_W + MEM_W
A_OUT_W = MIX_W + MEM_W
LNX_EPS = 64e-5

B_HD = 128
B_GROUPS = ((128, 1), (512, 4), (2048, 16))
B_HEADS = MIX_W // B_HD
B_GH = B_HEADS // len(B_GROUPS)
B_OUT_W = B_GH * B_HD
B_IN_W = 3 * MIX_W + MEM_W
B_OUT_IN = B_OUT_W + MEM_W

D_FF = -(-(8 * D_MODEL) // (3 * 256)) * 256

kernel_name = 'rwkv7_dilated_swa_memxattn_hybrid_step'


def _rmsnorm(x, g, eps=RMS_EPS):
    xf = x.astype(jnp.float32)
    y = xf * lax.rsqrt(jnp.mean(xf * xf, axis=-1, keepdims=True) + eps)
    return (y * g.astype(jnp.float32)).astype(x.dtype)


def _mem_kv(mem, g_norm, w_kv, g_k):
    b = mem.shape[0]
    kv = (_rmsnorm(mem, g_norm) @ w_kv).reshape(b, N_MEM, 2, MEM_HEADS, MEM_HD)
    return jnp.stack([_rmsnorm(kv[:, :, 0], g_k), kv[:, :, 1]], axis=2)


def _mem_attend(qm, kv, g_q):
    b, t = qm.shape[:2]
    q = _rmsnorm(qm.reshape(b, t, MEM_HEADS, MEM_HD), g_q)
    s = jnp.einsum('bthd,bmhd->bhtm', q, kv[:, :, 0]).astype(jnp.float32) * (MEM_HD ** -0.5)
    p = jax.nn.softmax(s, axis=-1).astype(kv.dtype)
    return jnp.einsum('bhtm,bmhd->bthd', p, kv[:, :, 1]).reshape(b, t, MEM_W)


def _wkv_scan(s0, r, w, k, v, a, b):
    xs = tuple(jnp.moveaxis(z.astype(jnp.float32), 1, 0) for z in (r, w, k, v, a, b))

    def step(s, inp):
        r_t, w_t, k_t, v_t, a_t, b_t = inp
        sa = jnp.einsum('bhvk,bhk->bhv', s, a_t)
        s = s * w_t[:, :, None, :] + sa[..., None] * b_t[:, :, None, :] + v_t[..., None] * k_t[:, :, None, :]
        return s, jnp.einsum('bhvk,bhk->bhv', s, r_t)

    s, ys = lax.scan(step, s0.astype(jnp.float32), xs)
    return s, jnp.moveaxis(ys, 0, 1)


def _rwkv7(u, u_prev0, s0, mu, w0, w2, a0, a2, g2, k_k, k_a, r_k, lnx_g, lnx_b):
    bsz, t = u.shape[:2]
    u_prev = jnp.concatenate([u_prev0.astype(u.dtype), u[:, :-1]], axis=1)
    us = u + (u_prev - u) * mu
    o = 3 * MIX_W
    r, k, v = us[..., :MIX_W], us[..., MIX_W:2 * MIX_W], us[..., 2 * MIX_W:o]
    wl = us[..., o:o + DECAY_LORA]
    al = us[..., o + DECAY_LORA:o + DECAY_LORA + AAA_LORA]
    gl = us[..., o + DECAY_LORA + AAA_LORA:]
    w_log = -jax.nn.softplus(-(w0 + jnp.tanh(wl) @ w2)) - 0.5
    decay = jnp.exp(-jnp.exp(w_log.astype(jnp.float32)))
    a = jax.nn.sigmoid(a0 + al @ a2)
    g = jax.nn.sigmoid(gl) @ g2
    hs = lambda z: z.reshape(bsz, t, A_HEADS, A_HD)
    kk = hs(k * k_k).astype(jnp.float32)
    kk = kk / jnp.maximum(jnp.sqrt(jnp.sum(kk * kk, axis=-1, keepdims=True)), 1e-12)
    k = k * (1 + (a - 1) * k_a)
    ah = hs(a).astype(jnp.float32)
    s, y = _wkv_scan(s0, hs(r), hs(decay), hs(k), hs(v), -kk, kk * ah)
    mean = jnp.mean(y, axis=-1, keepdims=True)
    var = jnp.mean(jnp.square(y - mean), axis=-1, keepdims=True)
    y = ((y - mean) * lax.rsqrt(var + LNX_EPS)).reshape(bsz, t, MIX_W) * lnx_g + lnx_b
    bonus = jnp.sum(hs(r).astype(jnp.float32) * hs(k).astype(jnp.float32) * r_k, axis=-1, keepdims=True) * hs(v).astype(jnp.float32)
    y = (y + bonus.reshape(bsz, t, MIX_W)).astype(u.dtype) * g
    return y, s, u[:, -1:]


def _dswa_qkv(u, g_q, g_k):
    bsz, t = u.shape[:2]
    hs = lambda z: z.reshape(bsz, t, B_HEADS, B_HD)
    q = _rmsnorm(hs(u[..., :MIX_W]), g_q)
    k = _rmsnorm(hs(u[..., MIX_W:2 * MIX_W]), g_k)
    v = hs(u[..., 2 * MIX_W:3 * MIX_W])
    return q, k, v


def _dilated_band_attn(q, k, v, window, dil):
    bsz, t = q.shape[:2]
    span = window // dil
    sub = t // dil
    n_blk = -(-sub // span)
    pad = n_blk * span - sub

    def blocks(z):
        z = z.reshape(bsz, sub, dil, B_GH, B_HD).transpose(0, 2, 1, 3, 4)
        z = jnp.pad(z, ((0, 0), (0, 0), (0, pad), (0, 0), (0, 0)))
        return z.reshape(bsz, dil, n_blk, span, B_GH, B_HD)

    def with_prev(z):
        prev = jnp.pad(z, ((0, 0), (0, 0), (1, 0), (0, 0), (0, 0), (0, 0)))[:, :, :-1]
        return jnp.concatenate([prev, z], axis=3)

    qb = blocks(q)
    kb = with_prev(blocks(k))
    vb = with_prev(blocks(v))
    s = jnp.einsum('brnqhd,brnkhd->brnhqk', qb, kb).astype(jnp.float32) * (B_HD ** -0.5)
    qi = jnp.arange(span)[:, None] + span
    ki = jnp.arange(2 * span)[None, :]
    dist = qi - ki
    band = (dist >= 0) & (dist <= span)
    exists = (jnp.arange(n_blk)[:, None, None] * span + ki[None] - span) >= 0
    mask = band[None] & exists
    s = jnp.where(mask[None, None, :, None], s, -jnp.inf)
    lse = jax.nn.logsumexp(s, axis=-1)
    p = jnp.exp(s - lse[..., None]).astype(v.dtype)
    o = jnp.einsum('brnhqk,brnkhd->brnqhd', p, vb)
    o = o.reshape(bsz, dil, n_blk * span, B_GH, B_HD)[:, :, :sub]
    o = o.transpose(0, 2, 1, 3, 4).reshape(bsz, t, B_GH, B_HD)
    lse = lse.transpose(0, 1, 2, 4, 3).reshape(bsz, dil, n_blk * span, B_GH)[:, :, :sub]
    lse = lse.transpose(0, 2, 1, 3).reshape(bsz, t, B_GH)
    return o, lse


def _dilated_window_step(q, k, v, buf, window, dil):
    rows = buf.shape[1]
    ts = q.shape[1]
    kc = jnp.concatenate([buf[:, :, 0], k], axis=1)
    vc = jnp.concatenate([buf[:, :, 1], v], axis=1)
    span = window // dil
    idx = rows + jnp.arange(ts)[:, None] - dil * jnp.arange(span + 1)[None, :]
    valid = idx >= 0
    idx = jnp.maximum(idx, 0)
    kg = kc[:, idx]
    vg = vc[:, idx]
    s = jnp.einsum('bthd,btjhd->bthj', q, kg).astype(jnp.float32) * (B_HD ** -0.5)
    s = jnp.where(valid[None, :, None, :], s, -jnp.inf)
    lse = jax.nn.logsumexp(s, axis=-1)
    p = jnp.exp(s - lse[..., None]).astype(vg.dtype)
    o = jnp.einsum('bthj,btjhd->bthd', p, vg)
    new_buf = jnp.stack([kc[:, -rows:], vc[:, -rows:]], axis=2)
    return o, lse, new_buf


def _merge_groups(outs, lses):
    wts = jax.nn.softmax(jnp.stack(lses, axis=0), axis=0)
    o = jnp.einsum('gbth,gbthd->bthd', wts, jnp.stack(outs, axis=0).astype(jnp.float32))
    return o.reshape(o.shape[0], o.shape[1], B_OUT_W)


def _dswa_prompt(u, g_q, g_k):
    q, k, v = _dswa_qkv(u, g_q, g_k)
    outs, lses, bufs = [], [], []
    for g, (win, dil) in enumerate(B_GROUPS):
        sl = slice(g * B_GH, (g + 1) * B_GH)
        o, l = _dilated_band_attn(q[:, :, sl], k[:, :, sl], v[:, :, sl], win, dil)
        outs.append(o)
        lses.append(l)
        keep = min(win, q.shape[1])
        bufs.append(jnp.stack([k[:, -keep:, sl], v[:, -keep:, sl]], axis=2))
    return _merge_groups(outs, lses).astype(u.dtype), bufs


def _dswa_sample(u, g_q, g_k, bufs_in):
    q, k, v = _dswa_qkv(u, g_q, g_k)
    outs, lses, bufs = [], [], []
    for g, (win, dil) in enumerate(B_GROUPS):
        sl = slice(g * B_GH, (g + 1) * B_GH)
        o, l, nb = _dilated_window_step(q[:, :, sl], k[:, :, sl], v[:, :, sl], bufs_in[g], win, dil)
        outs.append(o)
        lses.append(l)
        bufs.append(nb)
    return _merge_groups(outs, lses).astype(u.dtype), bufs


def _swiglu(h, w_in, w_out):
    gu = h @ w_in
    return (jax.nn.silu(gu[..., :D_FF]) * gu[..., D_FF:]) @ w_out


def setup_inputs(seed: int = 0) -> dict:
    key = jax.random.key(seed)
    ks = iter(jax.random.split(key, 64))
    f32 = jnp.float32

    def nrm(shape, scale=1.0):
        return jax.random.normal(next(ks), shape, f32) * scale

    def gain(shape):
        return 1.0 + nrm(shape, 0.02)

    rows = [min(w, PAST_LEN) for w, _ in B_GROUPS]
    return {
        'x_prompt': nrm((BATCH, SEQ, D_MODEL)),
        'x_sample': nrm((DEC_BATCH, DEC_SEQ, D_MODEL)),
        'state_wkv': nrm((N_A_LAYERS, DEC_BATCH, A_HEADS, A_HD, A_HD)),
        'state_shift': nrm((N_A_LAYERS, DEC_BATCH, 1, A_SHIFT_W)),
        'cache_swa_kv1': nrm((N_B_LAYERS, DEC_BATCH, rows[0], 2, B_GH, B_HD)),
        'cache_swa_kv2': nrm((N_B_LAYERS, DEC_BATCH, rows[1], 2, B_GH, B_HD)),
        'cache_swa_kv3': nrm((N_B_LAYERS, DEC_BATCH, rows[2], 2, B_GH, B_HD)),
        'cache_mem_kv': nrm((DEPTH, DEC_BATCH, N_MEM, 2, MEM_HEADS, MEM_HD)),
        'mem_prompt': nrm((BATCH, N_MEM, D_MODEL)),
        'norm_mix': gain((DEPTH, D_MODEL)),
        'norm_ffn': gain((DEPTH, D_MODEL)),
        'norm_mem': gain((DEPTH, D_MODEL)),
        'w_mem_kv': nrm((DEPTH, D_MODEL, 2 * MEM_W), D_MODEL ** -0.5),
        'q_norm_mem': gain((DEPTH, MEM_HD)),
        'k_norm_mem': gain((DEPTH, MEM_HD)),
        'w_in_a': nrm((N_A_LAYERS, D_MODEL, A_IN_W), D_MODEL ** -0.5),
        'w_out_a': nrm((N_A_LAYERS, A_OUT_W, D_MODEL), A_OUT_W ** -0.5),
        'mu_a': jax.random.uniform(next(ks), (N_A_LAYERS, A_SHIFT_W), f32),
        'w0_a': -2.0 + nrm((N_A_LAYERS, MIX_W), 0.5),
        'w2_a': nrm((N_A_LAYERS, DECAY_LORA, MIX_W), 0.1),
        'a0_a': nrm((N_A_LAYERS, MIX_W), 0.5),
        'a2_a': nrm((N_A_LAYERS, AAA_LORA, MIX_W), 0.1),
        'g2_a': nrm((N_A_LAYERS, GATE_LORA, MIX_W), GATE_LORA ** -0.5),
        'kk_a': 0.85 + nrm((N_A_LAYERS, MIX_W), 0.05),
        'ka_a': 1.0 + nrm((N_A_LAYERS, MIX_W), 0.05),
        'rk_a': nrm((N_A_LAYERS, A_HEADS, A_HD), 0.1),
        'lnx_g_a': gain((N_A_LAYERS, MIX_W)),
        'lnx_b_a': nrm((N_A_LAYERS, MIX_W), 0.02),
        'w_in_b': nrm((N_B_LAYERS, D_MODEL, B_IN_W), D_MODEL ** -0.5),
        'w_out_b': nrm((N_B_LAYERS, B_OUT_IN, D_MODEL), B_OUT_IN ** -0.5),
        'q_norm_b': gain((N_B_LAYERS, B_HD)),
        'k_norm_b': gain((N_B_LAYERS, B_HD)),
        'w_ffn_in': nrm((DEPTH, D_MODEL, 2 * D_FF), D_MODEL ** -0.5),
        'w_ffn_out': nrm((DEPTH, D_FF, D_MODEL), D_FF ** -0.5),
    }


def reference(x_prompt, x_sample, state_wkv, state_shift, cache_swa_kv1, cache_swa_kv2, cache_swa_kv3,
              cache_mem_kv, mem_prompt, norm_mix, norm_ffn, norm_mem, w_mem_kv, q_norm_mem, k_norm_mem,
              w_in_a, w_out_a, mu_a, w0_a, w2_a, a0_a, a2_a, g2_a, kk_a, ka_a, rk_a, lnx_g_a, lnx_b_a,
              w_in_b, w_out_b, q_norm_b, k_norm_b, w_ffn_in, w_ffn_out):
    yp, ys = x_prompt, x_sample
    bp = x_prompt.shape[0]
    swa_in = (cache_swa_kv1, cache_swa_kv2, cache_swa_kv3)
    wkv_p, shift_p, wkv_s, shift_s, mem_p = [], [], [], [], []
    swa_p = [[] for _ in B_GROUPS]
    swa_s = [[] for _ in B_GROUPS]
    for i in range(DEPTH):
        j = i // N_MIXERS
        mkv_p = _mem_kv(mem_prompt, norm_mem[i], w_mem_kv[i], k_norm_mem[i])
        mem_p.append(mkv_p)
        mkv_s = cache_mem_kv[i]
        hp = _rmsnorm(yp, norm_mix[i])
        hs_ = _rmsnorm(ys, norm_mix[i])
        if i % N_MIXERS == 0:
            up = hp @ w_in_a[j]
            us = hs_ @ w_in_a[j]
            prm = (mu_a[j], w0_a[j], w2_a[j], a0_a[j], a2_a[j], g2_a[j], kk_a[j], ka_a[j], rk_a[j], lnx_g_a[j], lnx_b_a[j])
            zero_row = jnp.zeros((bp, 1, A_SHIFT_W), up.dtype)
            zero_s = jnp.zeros((bp, A_HEADS, A_HD, A_HD), jnp.float32)
            mp, sp, lp = _rwkv7(up[..., :A_SHIFT_W], zero_row, zero_s, *prm)
            ms, ss, ls = _rwkv7(us[..., :A_SHIFT_W], state_shift[j], state_wkv[j], *prm)
            op = jnp.concatenate([mp, _mem_attend(up[..., A_SHIFT_W:], mkv_p, q_norm_mem[i])], axis=-1)
            os_ = jnp.concatenate([ms, _mem_attend(us[..., A_SHIFT_W:], mkv_s, q_norm_mem[i])], axis=-1)
            yp = yp + op @ w_out_a[j]
            ys = ys + os_ @ w_out_a[j]
            wkv_p.append(sp)
            shift_p.append(lp)
            wkv_s.append(ss)
            shift_s.append(ls)
        else:
            up = hp @ w_in_b[j]
            us = hs_ @ w_in_b[j]
            mp, bufs_p = _dswa_prompt(up, q_norm_b[j], k_norm_b[j])
            ms, bufs_s = _dswa_sample(us, q_norm_b[j], k_norm_b[j], [c[j] for c in swa_in])
            op = jnp.concatenate([mp, _mem_attend(up[..., 3 * MIX_W:], mkv_p, q_norm_mem[i])], axis=-1)
            os_ = jnp.concatenate([ms, _mem_attend(us[..., 3 * MIX_W:], mkv_s, q_norm_mem[i])], axis=-1)
            yp = yp + op @ w_out_b[j]
            ys = ys + os_ @ w_out_b[j]
            for g in range(len(B_GROUPS)):
                swa_p[g].append(bufs_p[g])
                swa_s[g].append(bufs_s[g])
        yp = yp + _swiglu(_rmsnorm(yp, norm_ffn[i]), w_ffn_in[i], w_ffn_out[i])
        ys = ys + _swiglu(_rmsnorm(ys, norm_ffn[i]), w_ffn_in[i], w_ffn_out[i])
    return (yp, ys,
            jnp.stack(wkv_p), jnp.stack(shift_p),
            jnp.stack(swa_p[0]), jnp.stack(swa_p[1]), jnp.stack(swa_p[2]),
            jnp.stack(mem_p),
            jnp.stack(wkv_s), jnp.stack(shift_s),
            jnp.stack(swa_s[0]), jnp.stack(swa_s[1]), jnp.stack(swa_s[2]))
```

```python
import functools

import jax
import jax.numpy as jnp
from jax import lax
from jax.experimental import pallas as pl
from jax.experimental.pallas import tpu as pltpu

F32 = jnp.float32
BF16 = jnp.bfloat16

RMS_EPS = 1e-6
LNX_EPS = 64e-5

LANES = 128
SUBLANES = 8
VMEM_LIMIT_BYTES = 56 * 2**20

MEM_HEADS = 4
A_HD = 64
B_HD = 128
B_GROUPS = ((128, 1), (512, 4), (2048, 16))
DECAY_LORA = 96
AAA_LORA = 96
GATE_LORA = 384
CHUNK = 64
LORA_PAD = 1024

NT = (((1,), (1,)), ((), ()))
TN = (((0,), (0,)), ((), ()))
NN = (((1,), (0,)), ((), ()))


def _cp(*sem):
    return pltpu.CompilerParams(dimension_semantics=sem, vmem_limit_bytes=VMEM_LIMIT_BYTES)


def _dot(a, b, dims=NN):
    return lax.dot_general(a.astype(BF16), b.astype(BF16), dims, preferred_element_type=F32)


def _dot2(a, b_exact, dims=NN):
    hi = a.astype(BF16)
    lo = (a - hi.astype(F32)).astype(BF16)
    bb = b_exact.astype(BF16)
    return (lax.dot_general(hi, bb, dims, preferred_element_type=F32)
            + lax.dot_general(lo, bb, dims, preferred_element_type=F32))


def _sigmoid(x):
    return 1.0 / (1.0 + jnp.exp(-x))


def _head_ones(n, hd):
    r = lax.broadcasted_iota(jnp.int32, (n, n), 0) // hd
    c = lax.broadcasted_iota(jnp.int32, (n, n), 1) // hd
    return (r == c).astype(F32)


def _rmsnorm_kernel(x_ref, g_ref, o_ref):
    x = x_ref[...]
    y = x * lax.rsqrt(jnp.mean(x * x, axis=-1, keepdims=True) + RMS_EPS)
    o_ref[...] = (y * g_ref[...]).astype(o_ref.dtype)


def _rmsnorm(x, g, tm):
    m, d = x.shape
    return pl.pallas_call(
        _rmsnorm_kernel,
        grid=(m // tm,),
        in_specs=[pl.BlockSpec((tm, d), lambda i: (i, 0)), pl.BlockSpec((1, d), lambda i: (0, 0))],
        out_specs=pl.BlockSpec((tm, d), lambda i: (i, 0)),
        out_shape=jax.ShapeDtypeStruct((m, d), BF16),
        compiler_params=_cp("parallel"),
        name="rmsnorm",
    )(x, g.reshape(1, d))


def _mm_kernel(x_ref, w_ref, o_ref):
    o_ref[...] = jnp.dot(x_ref[...], w_ref[...], preferred_element_type=F32).astype(o_ref.dtype)


def _mm_res_kernel(x_ref, w_ref, r_ref, o_ref):
    acc = jnp.dot(x_ref[...], w_ref[...], preferred_element_type=F32)
    o_ref[...] = (acc + r_ref[...]).astype(o_ref.dtype)


def _matmul(x, w, *, tm, tn, res=None, out_dtype=F32, name="matmul"):
    m, k = x.shape
    n = w.shape[1]
    in_specs = [pl.BlockSpec((tm, k), lambda i, j: (i, 0)), pl.BlockSpec((k, tn), lambda i, j: (0, j))]
    args = [x, w]
    kern = _mm_kernel
    if res is not None:
        in_specs.append(pl.BlockSpec((tm, tn), lambda i, j: (i, j)))
        args.append(res)
        kern = _mm_res_kernel
    return pl.pallas_call(
        kern,
        grid=(m // tm, n // tn),
        in_specs=in_specs,
        out_specs=pl.BlockSpec((tm, tn), lambda i, j: (i, j)),
        out_shape=jax.ShapeDtypeStruct((m, n), out_dtype),
        compiler_params=_cp("parallel", "arbitrary"),
        name=name,
    )(*args)


def _ffn_in_kernel(x_ref, wg_ref, wu_ref, o_ref):
    x = x_ref[...]
    g = jnp.dot(x, wg_ref[...], preferred_element_type=F32)
    u = jnp.dot(x, wu_ref[...], preferred_element_type=F32)
    o_ref[...] = (g * _sigmoid(g) * u).astype(o_ref.dtype)


def _ffn_in(x, w, *, tm, tn):
    m, k = x.shape
    f = w.shape[1] // 2
    nf = f // tn
    return pl.pallas_call(
        _ffn_in_kernel,
        grid=(m // tm, nf),
        in_specs=[pl.BlockSpec((tm, k), lambda i, j: (i, 0)),
                  pl.BlockSpec((k, tn), lambda i, j: (0, j)),
                  pl.BlockSpec((k, tn), lambda i, j: (0, nf + j))],
        out_specs=pl.BlockSpec((tm, tn), lambda i, j: (i, j)),
        out_shape=jax.ShapeDtypeStruct((m, f), BF16),
        compiler_params=_cp("parallel", "arbitrary"),
        name="ffn_in",
    )(x, w, w)


def _ffn(y, g, w_in, w_out, *, tm):
    h = _rmsnorm(y, g, min(tm, 512))
    act = _ffn_in(h, w_in, tm=tm, tn=256)
    tm2 = min(tm, 512)
    return _matmul(act, w_out, tm=tm2, tn=512, res=y, name="ffn_out")


def _memkv_kernel(x_ref, g_ref, o_ref, *, heads, hd):
    w = heads * hd
    for h in range(heads):
        xh = x_ref[:, h * hd:(h + 1) * hd]
        o_ref[:, h * hd:(h + 1) * hd] = xh * lax.rsqrt(jnp.mean(xh * xh, axis=-1, keepdims=True) + RMS_EPS) * g_ref[...]
    o_ref[:, w:] = x_ref[:, w:]


def _mem_kv(mem2d, g_norm, w_kv, g_k, *, hd):
    m = mem2d.shape[0]
    kv = _matmul(_rmsnorm(mem2d, g_norm, 512), w_kv, tm=m, tn=1024, name="mem_kv")
    n = kv.shape[1]
    tm = 256
    return pl.pallas_call(
        functools.partial(_memkv_kernel, heads=MEM_HEADS, hd=hd),
        grid=(m // tm,),
        in_specs=[pl.BlockSpec((tm, n), lambda i: (i, 0)), pl.BlockSpec((1, hd), lambda i: (0, 0))],
        out_specs=pl.BlockSpec((tm, n), lambda i: (i, 0)),
        out_shape=jax.ShapeDtypeStruct((m, n), F32),
        compiler_params=_cp("parallel"),
        name="mem_kv_norm",
    )(kv, g_k.reshape(1, hd))


def _mem_attend_kernel(q_ref, kv_ref, g_ref, o_ref, *, heads, hd):
    w = heads * hd
    for h in range(heads):
        q = q_ref[:, h * hd:(h + 1) * hd]
        qn = q * lax.rsqrt(jnp.mean(q * q, axis=-1, keepdims=True) + RMS_EPS) * g_ref[...]
        k = kv_ref[0, :, h * hd:(h + 1) * hd]
        v = kv_ref[0, :, w + h * hd:w + (h + 1) * hd]
        s = _dot(qn, k, NT) * (hd ** -0.5)
        p = jnp.exp(s - jnp.max(s, axis=-1, keepdims=True))
        o = _dot(p, v) / jnp.sum(p, axis=-1, keepdims=True)
        o_ref[:, h * hd:(h + 1) * hd] = o.astype(o_ref.dtype)


def _mem_attend(u, col_block, kv, g_q, *, t, tq, out_dtype):
    m = u.shape[0]
    n_mem, w2 = kv.shape[1:]
    w = w2 // 2
    hd = w // MEM_HEADS
    tpb = t // tq
    return pl.pallas_call(
        functools.partial(_mem_attend_kernel, heads=MEM_HEADS, hd=hd),
        grid=(m // tq,),
        in_specs=[pl.BlockSpec((tq, w), lambda i: (i, col_block)),
                  pl.BlockSpec((1, n_mem, w2), lambda i: (i // tpb, 0, 0)),
                  pl.BlockSpec((1, hd), lambda i: (0, 0))],
        out_specs=pl.BlockSpec((tq, w), lambda i: (i, 0)),
        out_shape=jax.ShapeDtypeStruct((m, w), out_dtype),
        compiler_params=_cp("parallel"),
        name="mem_attend",
    )(u, kv, g_q.reshape(1, hd))


def _rwkv_prep_kernel(ur, uk, uv, ul, pr, pk, pv, plo, zr, zk, zv, zl, mur, muk, muv, mul,
                      w0, a0, kkp, kap, w2, a2, g2,
                      r_o, lw_o, k_o, v_o, a_o, b_o, g_o, *, tiles_per_batch):
    first = (pl.program_id(0) % tiles_per_batch) == 0

    def shift(x_ref, p_ref, z_ref, mu_ref):
        x = x_ref[...]
        prev_row = jnp.where(first, z_ref[0], p_ref[SUBLANES - 1:SUBLANES, :])
        xp = pltpu.roll(x, 1, axis=0)
        row = lax.broadcasted_iota(jnp.int32, x.shape, 0)
        xp = jnp.where(row == 0, prev_row, xp)
        return x + (xp - x) * mu_ref[...]

    r = shift(ur, pr, zr, mur)
    k = shift(uk, pk, zk, muk)
    v = shift(uv, pv, zv, muv)
    lo = shift(ul, plo, zl, mul)
    wl = lo[:, 0:LANES]
    al = lo[:, LANES:2 * LANES]
    gl = lo[:, 2 * LANES:2 * LANES + GATE_LORA]

    z = -(w0[...] + _dot(jnp.tanh(wl), w2[...]))
    softplus = jnp.maximum(z, 0.0) + jnp.log(1.0 + jnp.exp(-jnp.abs(z)))
    lw_o[...] = -jnp.exp(-softplus - 0.5)
    a = _sigmoid(a0[...] + _dot(al, a2[...]))
    g_o[...] = _dot(_sigmoid(gl), g2[...])

    kk = k * kkp[...]
    tc = kk.shape[1]
    ones = _head_ones(LANES, A_HD)
    parts = []
    for c in range(tc // LANES):
        sq = kk[:, c * LANES:(c + 1) * LANES]
        parts.append(_dot2(sq * sq, ones))
    ss = jnp.concatenate(parts, axis=1) if len(parts) > 1 else parts[0]
    kkn = kk / jnp.maximum(jnp.sqrt(ss), 1e-12)
    r_o[...] = r
    k_o[...] = k * (1.0 + (a - 1.0) * kap[...])
    v_o[...] = v
    a_o[...] = -kkn
    b_o[...] = kkn * a


def _rwkv_prep(u, prev0, mu_p, w0, a0, kk, ka, w2p, a2p, g2, *, t, tt, tc, c):
    m = u.shape[0]
    nc = c // tc
    lb = (3 * c) // LORA_PAD
    tpb = t // tt
    rb = tt // SUBLANES

    def prev_idx(i):
        return jnp.maximum(i * rb - 1, 0)

    row1 = lambda a: a.reshape(1, -1)
    in_specs = [
        pl.BlockSpec((tt, tc), lambda i, j: (i, j)),
        pl.BlockSpec((tt, tc), lambda i, j: (i, nc + j)),
        pl.BlockSpec((tt, tc), lambda i, j: (i, 2 * nc + j)),
        pl.BlockSpec((tt, LORA_PAD), lambda i, j: (i, lb)),
        pl.BlockSpec((SUBLANES, tc), lambda i, j: (prev_idx(i), j)),
        pl.BlockSpec((SUBLANES, tc), lambda i, j: (prev_idx(i), nc + j)),
        pl.BlockSpec((SUBLANES, tc), lambda i, j: (prev_idx(i), 2 * nc + j)),
        pl.BlockSpec((SUBLANES, LORA_PAD), lambda i, j: (prev_idx(i), lb)),
        pl.BlockSpec((1, 1, tc), lambda i, j: (i // tpb, 0, j)),
        pl.BlockSpec((1, 1, tc), lambda i, j: (i // tpb, 0, nc + j)),
        pl.BlockSpec((1, 1, tc), lambda i, j: (i // tpb, 0, 2 * nc + j)),
        pl.BlockSpec((1, 1, LORA_PAD), lambda i, j: (i // tpb, 0, lb)),
        pl.BlockSpec((1, tc), lambda i, j: (0, j)),
        pl.BlockSpec((1, tc), lambda i, j: (0, nc + j)),
        pl.BlockSpec((1, tc), lambda i, j: (0, 2 * nc + j)),
        pl.BlockSpec((1, LORA_PAD), lambda i, j: (0, lb)),
        pl.BlockSpec((1, tc), lambda i, j: (0, j)),
        pl.BlockSpec((1, tc), lambda i, j: (0, j)),
        pl.BlockSpec((1, tc), lambda i, j: (0, j)),
        pl.BlockSpec((1, tc), lambda i, j: (0, j)),
        pl.BlockSpec((LANES, tc), lambda i, j: (0, j)),
        pl.BlockSpec((LANES, tc), lambda i, j: (0, j)),
        pl.BlockSpec((GATE_LORA, tc), lambda i, j: (0, j)),
    ]
    out_spec = pl.BlockSpec((tt, tc), lambda i, j: (i, j))
    out_sd = jax.ShapeDtypeStruct((m, c), F32)
    mu2 = row1(mu_p)
    return pl.pallas_call(
        functools.partial(_rwkv_prep_kernel, tiles_per_batch=tpb),
        grid=(m // tt, nc),
        in_specs=in_specs,
        out_specs=[out_spec] * 7,
        out_shape=[out_sd] * 7,
        compiler_params=_cp("parallel", "parallel"),
        name="rwkv_prep",
    )(u, u, u, u, u, u, u, u, prev0, prev0, prev0, prev0, mu2, mu2, mu2, mu2,
      row1(w0), row1(a0), row1(kk), row1(ka), w2p, a2p, g2)


def _rwkv_scan_kernel(r_ref, lw_ref, k_ref, v_ref, a_ref, b_ref, s0_ref, y_ref, st_ref, s_scr, *, hp):
    c = pl.program_id(2)
    n_chunks = pl.num_programs(2)
    ln = r_ref.shape[1]
    l2 = 2 * ln
    hd = A_HD

    lane = lax.broadcasted_iota(jnp.int32, (1, LANES), 1)
    m0 = (lane < hd).astype(F32)
    m1 = 1.0 - m0
    row = lax.broadcasted_iota(jnp.int32, (ln, ln), 0)
    col = lax.broadcasted_iota(jnp.int32, (ln, ln), 1)
    tril_incl = (col <= row).astype(F32)
    row2 = lax.broadcasted_iota(jnp.int32, (l2, l2), 0)
    col2 = lax.broadcasted_iota(jnp.int32, (l2, l2), 1)
    same = (row2 // ln) == (col2 // ln)
    strict2 = same & (col2 < row2)
    incl2 = same & (col2 <= row2)
    eye2 = (row2 == col2).astype(F32)
    rowk = lax.broadcasted_iota(jnp.int32, (LANES, LANES), 0)
    colk = lax.broadcasted_iota(jnp.int32, (LANES, LANES), 1)
    eyek = (rowk == colk).astype(F32)

    def stack(x):
        return jnp.concatenate([x * m0, x * m1], axis=0)

    def unstack(xs):
        return xs[0:ln] + xs[ln:l2]

    @pl.when(c == 0)
    def _():
        s_scr[...] = s0_ref[0]

    for p in range(hp):
        sl = slice(p * LANES, (p + 1) * LANES)
        r = r_ref[0, :, sl]
        lw = lw_ref[0, :, sl]
        k = k_ref[0, :, sl]
        v = v_ref[0, :, sl]
        a = a_ref[0, :, sl]
        b = b_ref[0, :, sl]

        cin = _cumsum(lw, tril_incl)
        cex = cin - lw
        cl = cin[ln - 1:ln, :]
        inv = jnp.exp(-cin)
        dec = jnp.exp(cl - cin)
        a_s = stack(a * jnp.exp(cex))
        r_s = stack(r * jnp.exp(cin))
        b_s = stack(b * inv)
        k_s = stack(k * inv)
        v_s = stack(v)
        bh_s = stack(b * dec)
        kh_s = stack(k * dec)

        g = _dot(jnp.concatenate([a_s, r_s], axis=0), jnp.concatenate([b_s, k_s], axis=0), NT)
        a_ab = jnp.where(strict2, g[0:l2, 0:l2], 0.0)
        a_ak = jnp.where(strict2, g[0:l2, l2:2 * l2], 0.0)
        r_b = jnp.where(incl2, g[l2:2 * l2, 0:l2], 0.0)
        r_k = jnp.where(incl2, g[l2:2 * l2, l2:2 * l2], 0.0)

        pw = a_ab
        tinv = eye2 + a_ab
        n_sq = ln.bit_length() - 2
        for _ in range(n_sq):
            pw = _dot(pw, pw)
            tinv = tinv + _dot(tinv, pw)

        av_s = _dot(a_ak, v_s)
        wz = _dot(tinv, jnp.concatenate([a_s, av_s], axis=1))
        w_s = wz[:, 0:LANES]
        z_s = wz[:, LANES:2 * LANES]
        q = unstack(r_s + _dot(r_b, w_s))
        y_loc = unstack(_dot(jnp.concatenate([r_b, r_k], axis=1), jnp.concatenate([z_s, v_s], axis=0)))
        mc = eyek * jnp.exp(cl) + _dot(w_s, bh_s, TN)
        nc = _dot(jnp.concatenate([z_s, v_s], axis=0), jnp.concatenate([bh_s, kh_s], axis=0), TN)

        s = s_scr[p]
        y_ref[0, :, sl] = _dot(q, s, NT) + y_loc
        s_scr[p] = _dot(s, mc) + nc

    @pl.when(c == n_chunks - 1)
    def _():
        st_ref[0] = s_scr[...]


def _cumsum(x, tril_incl):
    hi = x.astype(BF16)
    lo = (x - hi.astype(F32)).astype(BF16)
    t = tril_incl.astype(BF16)
    return (lax.dot_general(t, hi, NN, preferred_element_type=F32)
            + lax.dot_general(t, lo, NN, preferred_element_type=F32))


def _rwkv_scan(r, lw, k, v, a, b, s0, *, hp):
    bsz, t, c = r.shape
    heads = c // A_HD
    pairs = heads // 2
    gw = hp * LANES
    s0p = s0.reshape(bsz, pairs, 2, A_HD, A_HD)
    zero = jnp.zeros_like(s0p[:, :, 0])
    s0_bd = jnp.concatenate([jnp.concatenate([s0p[:, :, 0], zero], axis=-1),
                             jnp.concatenate([zero, s0p[:, :, 1]], axis=-1)], axis=-2)
    blk = pl.BlockSpec((1, CHUNK, gw), lambda bi, gi, ci: (bi, ci, gi))
    st_blk = pl.BlockSpec((1, hp, LANES, LANES), lambda bi, gi, ci: (bi, gi, 0, 0))
    y, st_bd = pl.pallas_call(
        functools.partial(_rwkv_scan_kernel, hp=hp),
        grid=(bsz, c // gw, t // CHUNK),
        in_specs=[blk] * 6 + [st_blk],
        out_specs=[blk, st_blk],
        out_shape=[jax.ShapeDtypeStruct((bsz, t, c), F32), jax.ShapeDtypeStruct((bsz, pairs, LANES, LANES), F32)],
        scratch_shapes=[pltpu.VMEM((hp, LANES, LANES), F32)],
        compiler_params=_cp("parallel", "parallel", "arbitrary"),
        name="rwkv_scan",
    )(r, lw, k, v, a, b, s0_bd)
    st = jnp.stack([st_bd[:, :, :A_HD, :A_HD], st_bd[:, :, A_HD:, A_HD:]], axis=2)
    return y, st.reshape(bsz, heads, A_HD, A_HD)


def _rwkv_post_kernel(y_ref, r_ref, k_ref, v_ref, g_ref, lg_ref, lb_ref, rk_ref, o_ref):
    tc = y_ref.shape[1]
    ones = _head_ones(LANES, A_HD)
    inv_hd = 1.0 / A_HD
    for c in range(tc // LANES):
        sl = slice(c * LANES, (c + 1) * LANES)
        y = y_ref[:, sl]
        mean = _dot2(y, ones) * inv_hd
        d = y - mean
        var = _dot2(d * d, ones) * inv_hd
        yn = d * lax.rsqrt(var + LNX_EPS) * lg_ref[:, sl] + lb_ref[:, sl]
        bonus = _dot2(r_ref[:, sl] * k_ref[:, sl] * rk_ref[:, sl], ones) * v_ref[:, sl]
        o_ref[:, sl] = ((yn + bonus) * g_ref[:, sl]).astype(o_ref.dtype)


def _rwkv_post(y, r, k, v, g, lnx_g, lnx_b, rk, *, tt, tc, out_dtype):
    m, c = y.shape
    blk = pl.BlockSpec((tt, tc), lambda i, j: (i, j))
    pblk = pl.BlockSpec((1, tc), lambda i, j: (0, j))
    return pl.pallas_call(
        _rwkv_post_kernel,
        grid=(m // tt, c // tc),
        in_specs=[blk] * 5 + [pblk] * 3,
        out_specs=blk,
        out_shape=jax.ShapeDtypeStruct((m, c), out_dtype),
        compiler_params=_cp("parallel", "parallel"),
        name="rwkv_post",
    )(y, r, k, v, g, lnx_g.reshape(1, c), lnx_b.reshape(1, c), rk.reshape(1, c))


def _pad_lora_cols(x, c):
    o = 3 * c
    z = lambda n: jnp.zeros(x.shape[:-1] + (n,), x.dtype)
    return jnp.concatenate([
        x[..., :o],
        x[..., o:o + DECAY_LORA], z(LANES - DECAY_LORA),
        x[..., o + DECAY_LORA:o + DECAY_LORA + AAA_LORA], z(LANES - AAA_LORA),
        x[..., o + DECAY_LORA + AAA_LORA:o + DECAY_LORA + AAA_LORA + GATE_LORA],
        z(LORA_PAD - 2 * LANES - GATE_LORA)], axis=-1)


def _unpad_lora_cols(x, c):
    o = 3 * c
    return jnp.concatenate([
        x[..., :o],
        x[..., o:o + DECAY_LORA],
        x[..., o + LANES:o + LANES + AAA_LORA],
        x[..., o + 2 * LANES:o + 2 * LANES + GATE_LORA]], axis=-1)


def _rwkv_mixer(u, prev0, s0, prm, *, bsz, t, tt, hp, out_dtype):
    mu_p, w0, a0, kk, ka, w2p, a2p, g2, rk, lnx_g, lnx_b = prm
    c = w0.shape[0]
    tc = 512
    r, lw, k, v, a, b, g = _rwkv_prep(u, prev0, mu_p, w0, a0, kk, ka, w2p, a2p, g2, t=t, tt=tt, tc=tc, c=c)
    tp = -(-t // CHUNK) * CHUNK
    sh = lambda z: z.reshape(bsz, t, c)
    if tp != t:
        padt = lambda z: jnp.pad(sh(z), ((0, 0), (0, tp - t), (0, 0)))
        y, st = _rwkv_scan(padt(r), padt(lw), padt(k), padt(v), padt(a), padt(b), s0, hp=hp)
        y = y[:, :t]
    else:
        y, st = _rwkv_scan(sh(r), sh(lw), sh(k), sh(v), sh(a), sh(b), s0, hp=hp)
    out = _rwkv_post(y.reshape(bsz * t, c), r, k, v, g, lnx_g, lnx_b, rk, tt=tt, tc=tc, out_dtype=out_dtype)
    return out, st


def _qk_norm(x, g):
    return x * lax.rsqrt(jnp.mean(x * x, axis=-1, keepdims=True) + RMS_EPS) * g


def _dswa_prompt_kernel(q0, k0, v0, q1, k1, v1, q2, k2, v2, gq_ref, gk_ref,
                        o_ref, kn0, kn1, kn2, qn_scr, og_scr, lse_scr):
    hd = q0.shape[1]
    t = q0.shape[0]
    scale = hd ** -0.5
    neg = -1e30
    groups = ((q0, k0, v0, kn0), (q1, k1, v1, kn1), (q2, k2, v2, kn2))
    for g, ((q_ref, k_ref, v_ref, kn_ref), (win, dil)) in enumerate(zip(groups, B_GROUPS)):
        span = win // dil
        nb = (t // dil) // span
        qn_scr[...] = _qk_norm(q_ref[...], gq_ref[...])
        kn_ref[...] = _qk_norm(k_ref[...], gk_ref[...])
        row = lax.broadcasted_iota(jnp.int32, (span, span), 0)
        col = lax.broadcasted_iota(jnp.int32, (span, span), 1)
        own_ok = col <= row
        prev_ok = col >= row

        def body(it, carry, dil=dil, span=span, nb=nb, g=g, v_ref=v_ref, kn_ref=kn_ref, own_ok=own_ok, prev_ok=prev_ok):
            res = it // nb
            blk = it % nb
            start = res + dil * span * blk
            pstart = jnp.maximum(start - dil * span, res)
            if dil == 1:
                cur = pl.ds(start, span)
                prv = pl.ds(pstart, span)
            else:
                cur = pl.ds(start, span, stride=dil)
                prv = pl.ds(pstart, span, stride=dil)
            qb = qn_scr[cur, :]
            s_c = _dot(qb, kn_ref[cur, :], NT) * scale
            s_p = _dot(qb, kn_ref[prv, :], NT) * scale
            s_c = jnp.where(own_ok, s_c, neg)
            s_p = jnp.where(prev_ok & (blk > 0), s_p, neg)
            m = jnp.maximum(jnp.max(s_c, axis=-1, keepdims=True), jnp.max(s_p, axis=-1, keepdims=True))
            p_c = jnp.exp(s_c - m)
            p_p = jnp.exp(s_p - m)
            l = jnp.sum(p_c, axis=-1, keepdims=True) + jnp.sum(p_p, axis=-1, keepdims=True)
            o = (_dot(p_c, v_ref[cur, :]) + _dot(p_p, v_ref[prv, :])) / l
            og_scr[g, cur, :] = o
            lse_scr[g, cur, :] = jnp.broadcast_to(m + jnp.log(l), (span, hd))
            return carry

        lax.fori_loop(0, dil * nb, body, 0)

    l0, l1, l2 = lse_scr[0], lse_scr[1], lse_scr[2]
    m = jnp.maximum(jnp.maximum(l0, l1), l2)
    e0, e1, e2 = jnp.exp(l0 - m), jnp.exp(l1 - m), jnp.exp(l2 - m)
    o = (e0 * og_scr[0] + e1 * og_scr[1] + e2 * og_scr[2]) / (e0 + e1 + e2)
    o_ref[...] = o.astype(o_ref.dtype)


def _dswa_prompt(u, g_q, g_k, *, bsz, t, gh):
    hd = B_HD
    ng = len(B_GROUPS)
    nh = ng * gh
    in_specs = []
    for g in range(ng):
        for part in range(3):
            in_specs.append(pl.BlockSpec((t, hd), lambda bi, h, g=g, part=part: (bi, part * nh + g * gh + h)))
    in_specs += [pl.BlockSpec((1, hd), lambda bi, h: (0, 0))] * 2
    oblk = pl.BlockSpec((t, hd), lambda bi, h: (bi, h))
    m = bsz * t
    return pl.pallas_call(
        _dswa_prompt_kernel,
        grid=(bsz, gh),
        in_specs=in_specs,
        out_specs=[oblk] * 4,
        out_shape=[jax.ShapeDtypeStruct((m, gh * hd), BF16)] + [jax.ShapeDtypeStruct((m, gh * hd), F32)] * 3,
        scratch_shapes=[pltpu.VMEM((t, hd), F32), pltpu.VMEM((ng, t, hd), F32), pltpu.VMEM((ng, t, hd), F32)],
        compiler_params=_cp("parallel", "parallel"),
        name="dswa_prompt",
    )(*([u] * 9), g_q.reshape(1, hd), g_k.reshape(1, hd))


def _dswa_sample_kernel(q0, k0, v0, q1, k1, v1, q2, k2, v2, ck0, cv0, ck1, cv1, ck2, cv2, gq_ref, gk_ref,
                        o_ref, nk0, nv0, nk1, nv1, nk2, nv2):
    hd = q0.shape[1]
    ts = q0.shape[0]
    scale = hd ** -0.5
    neg = -1e30
    groups = ((q0, k0, v0, ck0, cv0, nk0, nv0), (q1, k1, v1, ck1, cv1, nk1, nv1), (q2, k2, v2, ck2, cv2, nk2, nv2))
    outs, lses = [], []
    for (q_ref, k_ref, v_ref, ck, cv, nk, nv), (win, dil) in zip(groups, B_GROUPS):
        rows = ck.shape[1]
        qn = _qk_norm(q_ref[...], gq_ref[...])
        kn = _qk_norm(k_ref[...], gk_ref[...])
        v = v_ref[...]
        kc = ck[0]
        vc = cv[0]
        s_o = _dot(qn, kc, NT) * scale
        s_n = _dot(qn, kn, NT) * scale
        d_o = rows + lax.broadcasted_iota(jnp.int32, (ts, rows), 0) - lax.broadcasted_iota(jnp.int32, (ts, rows), 1)
        d_n = lax.broadcasted_iota(jnp.int32, (ts, ts), 0) - lax.broadcasted_iota(jnp.int32, (ts, ts), 1)
        ok_o = (d_o <= win) & ((d_o % dil) == 0)
        ok_n = (d_n >= 0) & ((d_n % dil) == 0)
        s_o = jnp.where(ok_o, s_o, neg)
        s_n = jnp.where(ok_n, s_n, neg)
        m = jnp.maximum(jnp.max(s_o, axis=-1, keepdims=True), jnp.max(s_n, axis=-1, keepdims=True))
        p_o = jnp.exp(s_o - m)
        p_n = jnp.exp(s_n - m)
        l = jnp.sum(p_o, axis=-1, keepdims=True) + jnp.sum(p_n, axis=-1, keepdims=True)
        outs.append((_dot(p_o, vc) + _dot(p_n, v)) / l)
        lses.append(m + jnp.log(l))
        nk[0, 0:rows - ts, :] = ck[0, ts:rows, :]
        nk[0, rows - ts:rows, :] = kn
        nv[0, 0:rows - ts, :] = cv[0, ts:rows, :]
        nv[0, rows - ts:rows, :] = v
    m = jnp.maximum(jnp.maximum(lses[0], lses[1]), lses[2])
    es = [jnp.exp(x - m) for x in lses]
    o = (es[0] * outs[0] + es[1] * outs[1] + es[2] * outs[2]) / (es[0] + es[1] + es[2])
    o_ref[...] = o.astype(o_ref.dtype)


def _dswa_sample(u, caches, g_q, g_k, *, bsz, ts):
    hd = B_HD
    ng = len(B_GROUPS)
    gh = caches[0].shape[3]
    nh = ng * gh
    in_specs = []
    for g in range(ng):
        for part in range(3):
            in_specs.append(pl.BlockSpec((ts, hd), lambda bi, h, g=g, part=part: (bi, part * nh + g * gh + h)))
    cache_args, k_specs, k_shapes = [], [], []
    for cch in caches:
        rows = cch.shape[1]
        flat = cch.reshape(bsz, rows, 2 * gh * hd)
        cache_args += [flat, flat]
        kblk = pl.BlockSpec((1, rows, hd), lambda bi, h: (bi, 0, h))
        vblk = pl.BlockSpec((1, rows, hd), lambda bi, h, gh=gh: (bi, 0, gh + h))
        in_specs += [kblk, vblk]
        k_specs += [kblk, kblk]
        k_shapes += [jax.ShapeDtypeStruct((bsz, rows, gh * hd), F32)] * 2
    in_specs += [pl.BlockSpec((1, hd), lambda bi, h: (0, 0))] * 2
    m = bsz * ts
    res = pl.pallas_call(
        _dswa_sample_kernel,
        grid=(bsz, gh),
        in_specs=in_specs,
        out_specs=[pl.BlockSpec((ts, hd), lambda bi, h: (bi, h))] + k_specs,
        out_shape=[jax.ShapeDtypeStruct((m, gh * hd), F32)] + k_shapes,
        compiler_params=_cp("parallel", "parallel"),
        name="dswa_sample",
    )(*([u] * 9), *cache_args, g_q.reshape(1, hd), g_k.reshape(1, hd))
    o = res[0]
    bufs = []
    for g, cch in enumerate(caches):
        rows = cch.shape[1]
        nk = res[1 + 2 * g].reshape(bsz, rows, gh, hd)
        nv = res[2 + 2 * g].reshape(bsz, rows, gh, hd)
        bufs.append(jnp.stack([nk, nv], axis=2))
    return o, bufs


def kernel(x_prompt, x_sample, state_wkv, state_shift, cache_swa_kv1, cache_swa_kv2, cache_swa_kv3, cache_mem_kv, mem_prompt, norm_mix, norm_ffn, norm_mem, w_mem_kv, q_norm_mem, k_norm_mem, w_in_a, w_out_a, mu_a, w0_a, w2_a, a0_a, a2_a, g2_a, kk_a, ka_a, rk_a, lnx_g_a, lnx_b_a, w_in_b, w_out_b, q_norm_b, k_norm_b, w_ffn_in, w_ffn_out):
    bp, t, d = x_prompt.shape
    bs, ts, _ = x_sample.shape
    n_mem = mem_prompt.shape[1]
    mem_w = w_mem_kv.shape[2] // 2
    mem_hd = mem_w // MEM_HEADS
    mix_w = w0_a.shape[1]
    depth = norm_mix.shape[0]
    swa_in = (cache_swa_kv1, cache_swa_kv2, cache_swa_kv3)
    tm = 1024

    yp = x_prompt.reshape(bp * t, d)
    ys = x_sample.reshape(bs * ts, d)
    mem2d = mem_prompt.reshape(bp * n_mem, d)

    wkv_p, shift_p, wkv_s, shift_s, mem_p = [], [], [], [], []
    swa_p = [[] for _ in B_GROUPS]
    swa_s = [[] for _ in B_GROUPS]
    for i in range(depth):
        j = i // 2
        mkv_p = _mem_kv(mem2d, norm_mem[i], w_mem_kv[i].astype(BF16), k_norm_mem[i], hd=mem_hd)
        mem_p.append(mkv_p.reshape(bp, n_mem, 2, MEM_HEADS, mem_hd))
        mkv_p = mkv_p.reshape(bp, n_mem, 2 * mem_w)
        mkv_s = cache_mem_kv[i].reshape(bs, n_mem, 2 * mem_w)
        hp_ = _rmsnorm(yp, norm_mix[i], 512)
        hs_ = _rmsnorm(ys, norm_mix[i], bs * ts)
        if i % 2 == 0:
            a_shift_w = mu_a.shape[1]
            w_in = jnp.concatenate([_pad_lora_cols(w_in_a[j][:, :a_shift_w], mix_w), w_in_a[j][:, a_shift_w:]], axis=1).astype(BF16)
            w_out = w_out_a[j].astype(BF16)
            up = _matmul(hp_, w_in, tm=tm, tn=1024, name="w_in_a")
            us = _matmul(hs_, w_in, tm=bs * ts, tn=1024, name="w_in_a_s")
            pad_rows = lambda w: jnp.pad(w, ((0, LANES - w.shape[0]), (0, 0)))
            prm = (_pad_lora_cols(mu_a[j], mix_w), w0_a[j], a0_a[j], kk_a[j], ka_a[j],
                   pad_rows(w2_a[j]), pad_rows(a2_a[j]), g2_a[j], rk_a[j].reshape(-1), lnx_g_a[j], lnx_b_a[j])
            wpad = 3 * mix_w + LORA_PAD
            mem_col = wpad // mem_w
            zero_row = jnp.zeros((bp, 1, wpad), F32)
            zero_s = jnp.zeros((bp, mix_w // A_HD, A_HD, A_HD), F32)
            mp, sp = _rwkv_mixer(up, zero_row, zero_s, prm, bsz=bp, t=t, tt=256, hp=4, out_dtype=BF16)
            ms, ss = _rwkv_mixer(us, _pad_lora_cols(state_shift[j], mix_w), state_wkv[j], prm,
                                 bsz=bs, t=ts, tt=ts, hp=4, out_dtype=F32)
            ap = _mem_attend(up, mem_col, mkv_p, q_norm_mem[i], t=t, tq=512, out_dtype=BF16)
            as_ = _mem_attend(us, mem_col, mkv_s, q_norm_mem[i], t=ts, tq=ts, out_dtype=F32)
            wkv_p.append(sp)
            wkv_s.append(ss)
            shift_p.append(_unpad_lora_cols(up.reshape(bp, t, -1)[:, -1:, :wpad], mix_w))
            shift_s.append(_unpad_lora_cols(us.reshape(bs, ts, -1)[:, -1:, :wpad], mix_w))
        else:
            w_in = w_in_b[j].astype(BF16)
            w_out = w_out_b[j].astype(BF16)
            up = _matmul(hp_, w_in, tm=tm, tn=1024, name="w_in_b")
            us = _matmul(hs_, w_in, tm=bs * ts, tn=1024, name="w_in_b_s")
            mem_col = (3 * mix_w) // mem_w
            gh = cache_swa_kv1.shape[4]
            mp, kn0, kn1, kn2 = _dswa_prompt(up, q_norm_b[j], k_norm_b[j], bsz=bp, t=t, gh=gh)
            ms, bufs_s = _dswa_sample(us, [c[j] for c in swa_in], q_norm_b[j], k_norm_b[j], bsz=bs, ts=ts)
            ap = _mem_attend(up, mem_col, mkv_p, q_norm_mem[i], t=t, tq=512, out_dtype=BF16)
            as_ = _mem_attend(us, mem_col, mkv_s, q_norm_mem[i], t=ts, tq=ts, out_dtype=F32)
            gh = cache_swa_kv1.shape[4]
            up4 = up.reshape(bp, t, -1)
            for g, ((win, _), kn) in enumerate(zip(B_GROUPS, (kn0, kn1, kn2))):
                keep = min(win, t)
                kk_ = kn.reshape(bp, t, gh, B_HD)[:, t - keep:]
                c0 = 2 * mix_w + g * gh * B_HD
                vv_ = up4[:, t - keep:, c0:c0 + gh * B_HD].reshape(bp, keep, gh, B_HD)
                swa_p[g].append(jnp.stack([kk_, vv_], axis=2))
                swa_s[g].append(bufs_s[g])
        op = jnp.concatenate([mp, ap], axis=-1)
        os_ = jnp.concatenate([ms.astype(BF16), as_.astype(BF16)], axis=-1)
        yp = _matmul(op, w_out, tm=tm, tn=1024, res=yp, name="w_out")
        ys = _matmul(os_, w_out, tm=bs * ts, tn=1024, res=ys, name="w_out_s")
        w_fi = w_ffn_in[i].astype(BF16)
        w_fo = w_ffn_out[i].astype(BF16)
        yp = _ffn(yp, norm_ffn[i], w_fi, w_fo, tm=tm)
        ys = _ffn(ys, norm_ffn[i], w_fi, w_fo, tm=bs * ts)
    return (yp.reshape(bp, t, d), ys.reshape(bs, ts, d),
            jnp.stack(wkv_p), jnp.stack(shift_p),
            jnp.stack(swa_p[0]), jnp.stack(swa_p[1]), jnp.stack(swa_p[2]),
            jnp.stack(mem_p),
            jnp.stack(wkv_s), jnp.stack(shift_s),
            jnp.stack(swa_s[0]), jnp.stack(swa_s[1]), jnp.stack(swa_s[2]))
```

```python
import functools

import jax
import jax.numpy as jnp
from jax import lax
from jax.experimental import pallas as pl
from jax.experimental.pallas import tpu as pltpu

F32 = jnp.float32
BF16 = jnp.bfloat16

RMS_EPS = 1e-6
LNX_EPS = 64e-5

LANES = 128
SUBLANES = 8
VMEM_LIMIT_BYTES = 56 * 2**20

MEM_HEADS = 4
A_HD = 64
B_HD = 128
B_GROUPS = ((128, 1), (512, 4), (2048, 16))
DECAY_LORA = 96
AAA_LORA = 96
GATE_LORA = 384
CHUNK = 64
LORA_PAD = 1024

NT = (((1,), (1,)), ((), ()))
TN = (((0,), (0,)), ((), ()))
NN = (((1,), (0,)), ((), ()))


def _cp(*sem):
    return pltpu.CompilerParams(dimension_semantics=sem, vmem_limit_bytes=VMEM_LIMIT_BYTES)


def _dot(a, b, dims=NN):
    return lax.dot_general(a.astype(BF16), b.astype(BF16), dims, preferred_element_type=F32)


def _sigmoid(x):
    return 1.0 / (1.0 + jnp.exp(-x))


def _row_tile(m):
    for parts in range(1, m // 16 + 1):
        if m % parts == 0 and (m // parts) % 16 == 0 and m // parts <= 1024:
            return m // parts
    raise ValueError(m)


def _rmsnorm_kernel(x_ref, g_ref, o_ref):
    x = x_ref[...]
    y = x * lax.rsqrt(jnp.mean(x * x, axis=-1, keepdims=True) + RMS_EPS)
    o_ref[...] = (y * g_ref[...]).astype(o_ref.dtype)


def _rmsnorm(x, g):
    m, d = x.shape
    tm = _row_tile(m)
    return pl.pallas_call(
        _rmsnorm_kernel,
        grid=(m // tm,),
        in_specs=[pl.BlockSpec((tm, d), lambda i: (i, 0)), pl.BlockSpec((1, d), lambda i: (0, 0))],
        out_specs=pl.BlockSpec((tm, d), lambda i: (i, 0)),
        out_shape=jax.ShapeDtypeStruct((m, d), BF16),
        compiler_params=_cp("parallel"),
        name="rmsnorm",
    )(x, g.reshape(1, d))


def _mm_ws_kernel(*refs, has_res):
    if has_res:
        x_ref, w_ref, r_ref, o_ref, wb_scr = refs
    else:
        x_ref, w_ref, o_ref, wb_scr = refs

    @pl.when(pl.program_id(1) == 0)
    def _():
        wb_scr[...] = w_ref[...].astype(BF16)

    acc = jnp.dot(x_ref[...], wb_scr[...], preferred_element_type=F32)
    if has_res:
        acc = acc + r_ref[...]
    o_ref[...] = acc.astype(o_ref.dtype)


def _matmul_ws(x, w, layer, *, n=None, tn=512, res=None, name="matmul"):
    m, k = x.shape
    n = w.shape[2] if n is None else n
    tm = _row_tile(m)
    in_specs = [pl.BlockSpec((tm, k), lambda j, i: (i, 0)),
                pl.BlockSpec((pl.Squeezed(), k, tn), lambda j, i: (layer, 0, j))]
    args = [x, w]
    if res is not None:
        in_specs.append(pl.BlockSpec((tm, tn), lambda j, i: (i, j)))
        args.append(res)
    return pl.pallas_call(
        functools.partial(_mm_ws_kernel, has_res=res is not None),
        grid=(n // tn, m // tm),
        in_specs=in_specs,
        out_specs=pl.BlockSpec((tm, tn), lambda j, i: (i, j)),
        out_shape=jax.ShapeDtypeStruct((m, n), F32),
        scratch_shapes=[pltpu.VMEM((k, tn), BF16)],
        compiler_params=_cp("arbitrary", "arbitrary"),
        name=name,
    )(*args)


def _ffn_in_kernel(x_ref, wg_ref, wu_ref, o_ref, wb_scr):
    tn = wg_ref.shape[1]

    @pl.when(pl.program_id(1) == 0)
    def _():
        wb_scr[:, 0:tn] = wg_ref[...].astype(BF16)
        wb_scr[:, tn:2 * tn] = wu_ref[...].astype(BF16)

    gu = jnp.dot(x_ref[...], wb_scr[...], preferred_element_type=F32)
    g = gu[:, 0:tn]
    o_ref[...] = (g * _sigmoid(g) * gu[:, tn:2 * tn]).astype(o_ref.dtype)


def _ffn_in(x, w, layer, *, tn=256):
    m, k = x.shape
    f = w.shape[2] // 2
    nf = f // tn
    tm = _row_tile(m)
    return pl.pallas_call(
        _ffn_in_kernel,
        grid=(nf, m // tm),
        in_specs=[pl.BlockSpec((tm, k), lambda j, i: (i, 0)),
                  pl.BlockSpec((pl.Squeezed(), k, tn), lambda j, i: (layer, 0, j)),
                  pl.BlockSpec((pl.Squeezed(), k, tn), lambda j, i: (layer, 0, nf + j))],
        out_specs=pl.BlockSpec((tm, tn), lambda j, i: (i, j)),
        out_shape=jax.ShapeDtypeStruct((m, f), BF16),
        scratch_shapes=[pltpu.VMEM((k, 2 * tn), BF16)],
        compiler_params=_cp("arbitrary", "arbitrary"),
        name="ffn_in",
    )(x, w, w)


def _mm_res_kernel(x_ref, w_ref, r_ref, o_ref):
    o_ref[...] = jnp.dot(x_ref[...], w_ref[...], preferred_element_type=F32) + r_ref[...]


def _ffn_out(x, w, res, *, tn=256):
    m, k = x.shape
    n = w.shape[1]
    tm = _row_tile(m)
    return pl.pallas_call(
        _mm_res_kernel,
        grid=(m // tm, n // tn),
        in_specs=[pl.BlockSpec((tm, k), lambda i, j: (i, 0)),
                  pl.BlockSpec((k, tn), lambda i, j: (0, j)),
                  pl.BlockSpec((tm, tn), lambda i, j: (i, j))],
        out_specs=pl.BlockSpec((tm, tn), lambda i, j: (i, j)),
        out_shape=jax.ShapeDtypeStruct((m, n), F32),
        compiler_params=_cp("parallel", "arbitrary"),
        name="ffn_out",
    )(x, w, res)


def _ffn(y, g, w_in, w_out, layer):
    act = _ffn_in(_rmsnorm(y, g), w_in, layer)
    return _ffn_out(act, w_out[layer].astype(BF16), y)


def _memkv_kernel(x_ref, g_ref, o_ref, *, heads, hd):
    w = heads * hd
    for h in range(heads):
        xh = x_ref[:, h * hd:(h + 1) * hd]
        o_ref[:, h * hd:(h + 1) * hd] = xh * lax.rsqrt(jnp.mean(xh * xh, axis=-1, keepdims=True) + RMS_EPS) * g_ref[...]
    o_ref[:, w:] = x_ref[:, w:]


def _mem_kv(mem2d, g_norm, w_kv, layer, g_k, *, hd):
    m = mem2d.shape[0]
    kv = _matmul_ws(_rmsnorm(mem2d, g_norm), w_kv, layer, name="mem_kv")
    n = kv.shape[1]
    tm = 256
    return pl.pallas_call(
        functools.partial(_memkv_kernel, heads=MEM_HEADS, hd=hd),
        grid=(m // tm,),
        in_specs=[pl.BlockSpec((tm, n), lambda i: (i, 0)), pl.BlockSpec((1, hd), lambda i: (0, 0))],
        out_specs=pl.BlockSpec((tm, n), lambda i: (i, 0)),
        out_shape=jax.ShapeDtypeStruct((m, n), F32),
        compiler_params=_cp("parallel"),
        name="mem_kv_norm",
    )(kv, g_k.reshape(1, hd))


def _mem_attend_kernel(*refs, heads, hd, aliased):
    q_ref, kv_ref, g_ref = refs[:3]
    o_ref = refs[-1]
    w = heads * hd
    for h in range(heads):
        q = q_ref[:, h * hd:(h + 1) * hd]
        qn = q * lax.rsqrt(jnp.mean(q * q, axis=-1, keepdims=True) + RMS_EPS) * g_ref[...]
        k = kv_ref[0, :, h * hd:(h + 1) * hd]
        v = kv_ref[0, :, w + h * hd:w + (h + 1) * hd]
        s = _dot(qn, k, NT) * (hd ** -0.5)
        p = jnp.exp(s - jnp.max(s, axis=-1, keepdims=True))
        o = _dot(p, v) / jnp.sum(p, axis=-1, keepdims=True)
        o_ref[:, h * hd:(h + 1) * hd] = o.astype(o_ref.dtype)


def _mem_attend(u, q_col, row0, kv, g_q, *, bsz, t, tq, into=None, out_col=0):
    n_mem, w2 = kv.shape[1:]
    w = w2 // 2
    hd = w // MEM_HEADS
    tpb = t // tq
    rb0 = row0 // tq
    in_specs = [pl.BlockSpec((tq, w), lambda i: (rb0 + i, q_col)),
                pl.BlockSpec((1, n_mem, w2), lambda i: (i // tpb, 0, 0)),
                pl.BlockSpec((1, hd), lambda i: (0, 0))]
    args = [u, kv, g_q.reshape(1, hd)]
    if into is None:
        out_shape = jax.ShapeDtypeStruct((bsz * t, w), F32)
        aliases = {}
    else:
        in_specs.append(pl.BlockSpec(memory_space=pl.ANY))
        args.append(into)
        out_shape = jax.ShapeDtypeStruct(into.shape, into.dtype)
        aliases = {3: 0}
    return pl.pallas_call(
        functools.partial(_mem_attend_kernel, heads=MEM_HEADS, hd=hd, aliased=into is not None),
        grid=(bsz * t // tq,),
        in_specs=in_specs,
        out_specs=pl.BlockSpec((tq, w), lambda i: (i, out_col)),
        out_shape=out_shape,
        input_output_aliases=aliases,
        compiler_params=_cp("parallel"),
        name="mem_attend",
    )(*args)


def _cumsum(x, tril_incl_bf16):
    hi = x.astype(BF16)
    lo = (x - hi.astype(F32)).astype(BF16)
    return (lax.dot_general(tril_incl_bf16, hi, NN, preferred_element_type=F32)
            + lax.dot_general(tril_incl_bf16, lo, NN, preferred_element_type=F32))


def _rwkv_kernel(ur, uk, uv, ul, zr, zk, zv, zl, mur, muk, muv, mul, w0, a0, kkp, kap, rkp, lg, lb, w2, a2, g2, s0_ref,
                 o_ref, st_ref, s_scr, pr_scr, pk_scr, pv_scr, pl_scr, *, hp, t_valid):
    c = pl.program_id(2)
    n_chunks = pl.num_programs(2)
    ln = ur.shape[0]
    l2 = 2 * ln
    hd = A_HD
    pairs = range(hp)

    @pl.when(c == 0)
    def _():
        s_scr[...] = s0_ref[0]
        pr_scr[...] = zr[0]
        pk_scr[...] = zk[0]
        pv_scr[...] = zv[0]
        pl_scr[...] = zl[0]

    def shift(x_ref, p_scr, mu_ref):
        x = x_ref[...]
        xp = pltpu.roll(x, 1, axis=0)
        row = lax.broadcasted_iota(jnp.int32, x.shape, 0)
        xp = jnp.where(row == 0, p_scr[...], xp)
        p_scr[...] = x[ln - 1:ln, :]
        return x + (xp - x) * mu_ref[...]

    r = shift(ur, pr_scr, mur)
    k = shift(uk, pk_scr, muk)
    v = shift(uv, pv_scr, muv)
    lo = shift(ul, pl_scr, mul)
    wl = lo[:, 0:LANES]
    al = lo[:, LANES:2 * LANES]
    gl = lo[:, 2 * LANES:2 * LANES + GATE_LORA]

    z = -(w0[...] + _dot(jnp.tanh(wl), w2[...]))
    lw = -jnp.exp(-(jnp.maximum(z, 0.0) + jnp.log(1.0 + jnp.exp(-jnp.abs(z)))) - 0.5)
    asig = _sigmoid(a0[...] + _dot(al, a2[...]))
    gate = _dot(_sigmoid(gl), g2[...])
    kk = k * kkp[...]
    k = k * (1.0 + (asig - 1.0) * kap[...])
    if t_valid < ln:
        valid = lax.broadcasted_iota(jnp.int32, (ln, 1), 0) < t_valid
        lw = jnp.where(valid, lw, 0.0)
        kk = jnp.where(valid, kk, 0.0)
        k = jnp.where(valid, k, 0.0)
        v = jnp.where(valid, v, 0.0)

    lane = lax.broadcasted_iota(jnp.int32, (1, LANES), 1)
    m0 = (lane < hd).astype(F32)
    m1 = 1.0 - m0
    row = lax.broadcasted_iota(jnp.int32, (ln, ln), 0)
    col = lax.broadcasted_iota(jnp.int32, (ln, ln), 1)
    tril_incl = (col <= row).astype(BF16)
    row2 = lax.broadcasted_iota(jnp.int32, (l2, l2), 0)
    col2 = lax.broadcasted_iota(jnp.int32, (l2, l2), 1)
    same = (row2 // ln) == (col2 // ln)
    strict2 = same & (col2 < row2)
    incl2 = same & (col2 <= row2)
    eye2 = (row2 == col2).astype(F32)
    rowk = lax.broadcasted_iota(jnp.int32, (LANES, LANES), 0)
    colk = lax.broadcasted_iota(jnp.int32, (LANES, LANES), 1)
    eyek = (rowk == colk).astype(F32)

    def sl(p):
        return slice(p * LANES, (p + 1) * LANES)

    def hsum(x):
        return (jnp.sum(x * m0, axis=-1, keepdims=True) * m0 + jnp.sum(x * m1, axis=-1, keepdims=True) * m1)

    def stack(x):
        return jnp.concatenate([x * m0, x * m1], axis=0)

    def unstack(xs):
        return xs[0:ln] + xs[ln:l2]

    r_ = [r[:, sl(p)] for p in pairs]
    k_ = [k[:, sl(p)] for p in pairs]
    v_ = [v[:, sl(p)] for p in pairs]
    lw_ = [lw[:, sl(p)] for p in pairs]
    kk_ = [kk[:, sl(p)] for p in pairs]
    kkn = [x / jnp.maximum(jnp.sqrt(hsum(x * x)), 1e-12) for x in kk_]
    a_ = [-x for x in kkn]
    b_ = [kkn[p] * asig[:, sl(p)] for p in pairs]

    cin = [_cumsum(x, tril_incl) for x in lw_]
    cl = [x[ln - 1:ln, :] for x in cin]
    inv = [jnp.exp(-x) for x in cin]
    dec = [jnp.exp(cl[p] - cin[p]) for p in pairs]
    a_s = [stack(a_[p] * jnp.exp(cin[p] - lw_[p])) for p in pairs]
    r_s = [stack(r_[p] * jnp.exp(cin[p])) for p in pairs]
    b_s = [stack(b_[p] * inv[p]) for p in pairs]
    k_s = [stack(k_[p] * inv[p]) for p in pairs]
    v_s = [stack(x) for x in v_]
    bh_s = [stack(b_[p] * dec[p]) for p in pairs]
    kh_s = [stack(k_[p] * dec[p]) for p in pairs]

    g = [_dot(jnp.concatenate([a_s[p], r_s[p]], axis=0), jnp.concatenate([b_s[p], k_s[p]], axis=0), NT) for p in pairs]
    a_ab = [jnp.where(strict2, x[0:l2, 0:l2], 0.0) for x in g]
    a_ak = [jnp.where(strict2, x[0:l2, l2:2 * l2], 0.0) for x in g]
    r_b = [jnp.where(incl2, x[l2:2 * l2, 0:l2], 0.0) for x in g]
    r_k = [jnp.where(incl2, x[l2:2 * l2, l2:2 * l2], 0.0) for x in g]

    pw = a_ab
    tinv = [eye2 + x for x in a_ab]
    for _ in range(ln.bit_length() - 2):
        pw = [_dot(x, x) for x in pw]
        tinv = [tinv[p] + _dot(tinv[p], pw[p]) for p in pairs]

    av_s = [_dot(a_ak[p], v_s[p]) for p in pairs]
    wz = [_dot(tinv[p], jnp.concatenate([a_s[p], av_s[p]], axis=1)) for p in pairs]
    w_s = [x[:, 0:LANES] for x in wz]
    z_s = [x[:, LANES:2 * LANES] for x in wz]
    zv_s = [jnp.concatenate([z_s[p], v_s[p]], axis=0) for p in pairs]
    q = [unstack(r_s[p] + _dot(r_b[p], w_s[p])) for p in pairs]
    y_loc = [unstack(_dot(jnp.concatenate([r_b[p], r_k[p]], axis=1), zv_s[p])) for p in pairs]
    mc = [eyek * jnp.exp(cl[p]) + _dot(w_s[p], bh_s[p], TN) for p in pairs]
    nc = [_dot(zv_s[p], jnp.concatenate([bh_s[p], kh_s[p]], axis=0), TN) for p in pairs]

    s = [s_scr[p] for p in pairs]
    y = [_dot(q[p], s[p], NT) + y_loc[p] for p in pairs]
    s_new = [_dot(s[p], mc[p]) + nc[p] for p in pairs]
    for p in pairs:
        s_scr[p] = s_new[p]

    inv_hd = 1.0 / hd
    mean = [hsum(x) * inv_hd for x in y]
    d = [y[p] - mean[p] for p in pairs]
    var = [hsum(x * x) * inv_hd for x in d]
    bonus = [hsum(r_[p] * k_[p] * rkp[:, sl(p)]) * v_[p] for p in pairs]
    for p in pairs:
        yn = d[p] * lax.rsqrt(var[p] + LNX_EPS) * lg[:, sl(p)] + lb[:, sl(p)]
        o_ref[:, sl(p)] = ((yn + bonus[p]) * gate[:, sl(p)]).astype(o_ref.dtype)

    @pl.when(c == n_chunks - 1)
    def _():
        st_ref[0] = s_scr[...]


def _lora_slot(x):
    z = lambda n: jnp.zeros(x.shape[:-1] + (n,), x.dtype)
    d, a = DECAY_LORA, AAA_LORA
    return jnp.concatenate([x[..., :d], z(LANES - d), x[..., d:d + a], z(LANES - a), x[..., d + a:],
                            z(LORA_PAD - 2 * LANES - GATE_LORA)], axis=-1)


def _lora_unslot(x):
    return jnp.concatenate([x[..., :DECAY_LORA], x[..., LANES:LANES + AAA_LORA],
                            x[..., 2 * LANES:2 * LANES + GATE_LORA]], axis=-1)


def _rwkv_mixer(u_main, u_tail, row0, prev_main, prev_lora, s0, prm, *, bsz, t, t_valid, hp, out_rows, out_cols):
    mu_main, mu_lora, w0, a0, kk, ka, rk, lnx_g, lnx_b, w2p, a2p, g2 = prm
    c = w0.shape[0]
    heads = c // A_HD
    n_pairs = heads // 2
    gw = hp * LANES
    ng = c // gw
    nch = t // CHUNK
    rb0 = row0 // CHUNK
    s0p = s0.reshape(bsz, n_pairs, 2, A_HD, A_HD)
    zero = jnp.zeros_like(s0p[:, :, 0])
    s0_bd = jnp.concatenate([jnp.concatenate([s0p[:, :, 0], zero], axis=-1),
                             jnp.concatenate([zero, s0p[:, :, 1]], axis=-1)], axis=-2)
    row1 = lambda a: a.reshape(1, -1)
    ublk = lambda off: pl.BlockSpec((CHUNK, gw), lambda bi, gi, ci: (rb0 + bi * nch + ci, off * ng + gi))
    zblk = lambda off: pl.BlockSpec((1, 1, gw), lambda bi, gi, ci: (bi, 0, off * ng + gi))
    pblk = lambda off: pl.BlockSpec((1, gw), lambda bi, gi, ci: (0, off * ng + gi))
    lora_c = pl.BlockSpec((1, LORA_PAD), lambda bi, gi, ci: (0, 0))
    st_blk = pl.BlockSpec((1, hp, LANES, LANES), lambda bi, gi, ci: (bi, gi, 0, 0))
    in_specs = [
        ublk(0), ublk(1), ublk(2),
        pl.BlockSpec((CHUNK, LORA_PAD), lambda bi, gi, ci: (rb0 + bi * nch + ci, 0)),
        zblk(0), zblk(1), zblk(2),
        pl.BlockSpec((1, 1, LORA_PAD), lambda bi, gi, ci: (bi, 0, 0)),
        pblk(0), pblk(1), pblk(2), lora_c,
        pblk(0), pblk(0), pblk(0), pblk(0), pblk(0), pblk(0), pblk(0),
        pl.BlockSpec((LANES, gw), lambda bi, gi, ci: (0, gi)),
        pl.BlockSpec((LANES, gw), lambda bi, gi, ci: (0, gi)),
        pl.BlockSpec((GATE_LORA, gw), lambda bi, gi, ci: (0, gi)),
        st_blk,
    ]
    out, st_bd = pl.pallas_call(
        functools.partial(_rwkv_kernel, hp=hp, t_valid=t_valid),
        grid=(bsz, ng, nch),
        in_specs=in_specs,
        out_specs=[pl.BlockSpec((CHUNK, gw), lambda bi, gi, ci: (bi * nch + ci, gi)), st_blk],
        out_shape=[jax.ShapeDtypeStruct((out_rows, out_cols), BF16),
                   jax.ShapeDtypeStruct((bsz, n_pairs, LANES, LANES), F32)],
        scratch_shapes=[pltpu.VMEM((hp, LANES, LANES), F32), pltpu.VMEM((1, gw), F32), pltpu.VMEM((1, gw), F32),
                        pltpu.VMEM((1, gw), F32), pltpu.VMEM((1, LORA_PAD), F32)],
        compiler_params=_cp("parallel", "parallel", "arbitrary"),
        name="rwkv_mixer",
    )(u_main, u_main, u_main, u_tail, prev_main, prev_main, prev_main, prev_lora,
      row1(mu_main), row1(mu_main), row1(mu_main), row1(mu_lora),
      row1(w0), row1(a0), row1(kk), row1(ka), row1(rk), row1(lnx_g), row1(lnx_b), w2p, a2p, g2, s0_bd)
    st = jnp.stack([st_bd[:, :, :A_HD, :A_HD], st_bd[:, :, A_HD:, A_HD:]], axis=2)
    return out, st.reshape(bsz, heads, A_HD, A_HD)


def _qk_norm(x, g):
    return x * lax.rsqrt(jnp.mean(x * x, axis=-1, keepdims=True) + RMS_EPS) * g


def _dswa_prompt_kernel(q0, k0, v0, q1, k1, v1, q2, k2, v2, gq_ref, gk_ref,
                        o_ref, c0, c1, c2, qn_scr, kn_scr, og_scr, lse_scr):
    hd = q0.shape[1]
    t = q0.shape[0]
    scale = hd ** -0.5
    neg = -1e30
    groups = ((q0, k0, v0, c0), (q1, k1, v1, c1), (q2, k2, v2, c2))
    for g, ((q_ref, k_ref, v_ref, c_ref), (win, dil)) in enumerate(zip(groups, B_GROUPS)):
        span = win // dil
        nb = (t // dil) // span
        keep = c_ref.shape[1] // 2
        qn_scr[...] = _qk_norm(q_ref[...], gq_ref[...])
        kn_scr[...] = _qk_norm(k_ref[...], gk_ref[...])
        c_ref[0, pl.ds(0, keep, stride=2), :] = kn_scr[t - keep:t, :]
        c_ref[0, pl.ds(1, keep, stride=2), :] = v_ref[t - keep:t, :]
        row = lax.broadcasted_iota(jnp.int32, (span, span), 0)
        col = lax.broadcasted_iota(jnp.int32, (span, span), 1)
        own_ok = col <= row
        prev_ok = col >= row

        def body(it, carry, dil=dil, span=span, nb=nb, g=g, v_ref=v_ref, own_ok=own_ok, prev_ok=prev_ok):
            res = it // nb
            blk = it % nb
            start = res + dil * span * blk
            pstart = jnp.maximum(start - dil * span, res)
            if dil == 1:
                cur = pl.ds(start, span)
                prv = pl.ds(pstart, span)
            else:
                cur = pl.ds(start, span, stride=dil)
                prv = pl.ds(pstart, span, stride=dil)
            qb = qn_scr[cur, :]
            s_c = _dot(qb, kn_scr[cur, :], NT) * scale
            s_p = _dot(qb, kn_scr[prv, :], NT) * scale
            s_c = jnp.where(own_ok, s_c, neg)
            s_p = jnp.where(prev_ok & (blk > 0), s_p, neg)
            m = jnp.maximum(jnp.max(s_c, axis=-1, keepdims=True), jnp.max(s_p, axis=-1, keepdims=True))
            p_c = jnp.exp(s_c - m)
            p_p = jnp.exp(s_p - m)
            l = jnp.sum(p_c, axis=-1, keepdims=True) + jnp.sum(p_p, axis=-1, keepdims=True)
            o = (_dot(p_c, v_ref[cur, :]) + _dot(p_p, v_ref[prv, :])) / l
            og_scr[g, cur, :] = o
            lse_scr[g, cur, :] = jnp.broadcast_to(m + jnp.log(l), (span, hd))
            return carry

        lax.fori_loop(0, dil * nb, body, 0)

    l0, l1, l2 = lse_scr[0], lse_scr[1], lse_scr[2]
    m = jnp.maximum(jnp.maximum(l0, l1), l2)
    e0, e1, e2 = jnp.exp(l0 - m), jnp.exp(l1 - m), jnp.exp(l2 - m)
    o = (e0 * og_scr[0] + e1 * og_scr[1] + e2 * og_scr[2]) / (e0 + e1 + e2)
    o_ref[...] = o.astype(o_ref.dtype)


def _dswa_prompt(u, g_q, g_k, *, bsz, t, gh, out_rows, out_cols):
    hd = B_HD
    ng = len(B_GROUPS)
    nh = ng * gh
    in_specs = []
    for g in range(ng):
        for part in range(3):
            in_specs.append(pl.BlockSpec((t, hd), lambda bi, h, g=g, part=part: (bi, part * nh + g * gh + h)))
    in_specs += [pl.BlockSpec((1, hd), lambda bi, h: (0, 0))] * 2
    keeps = [min(win, t) for win, _ in B_GROUPS]
    res = pl.pallas_call(
        _dswa_prompt_kernel,
        grid=(bsz, gh),
        in_specs=in_specs,
        out_specs=[pl.BlockSpec((t, hd), lambda bi, h: (bi, h))]
        + [pl.BlockSpec((1, 2 * kp, hd), lambda bi, h: (bi, 0, h)) for kp in keeps],
        out_shape=[jax.ShapeDtypeStruct((out_rows, out_cols), BF16)]
        + [jax.ShapeDtypeStruct((bsz, 2 * kp, gh * hd), F32) for kp in keeps],
        scratch_shapes=[pltpu.VMEM((t, hd), F32), pltpu.VMEM((t, hd), F32),
                        pltpu.VMEM((ng, t, hd), F32), pltpu.VMEM((ng, t, hd), F32)],
        compiler_params=_cp("parallel", "parallel"),
        name="dswa_prompt",
    )(*([u] * 9), g_q.reshape(1, hd), g_k.reshape(1, hd))
    return res[0], [b.reshape(bsz, kp, 2, gh, hd) for b, kp in zip(res[1:], keeps)]


def _dswa_sample_kernel(q0, k0, v0, q1, k1, v1, q2, k2, v2, c0, c1, c2, gq_ref, gk_ref, o_ref, n0, n1, n2):
    hd = q0.shape[1]
    ts = q0.shape[0]
    scale = hd ** -0.5
    neg = -1e30
    groups = ((q0, k0, v0, c0, n0), (q1, k1, v1, c1, n1), (q2, k2, v2, c2, n2))
    outs, lses = [], []
    for (q_ref, k_ref, v_ref, c_ref, n_ref), (win, dil) in zip(groups, B_GROUPS):
        rows = c_ref.shape[1] // 2
        qn = _qk_norm(q_ref[...], gq_ref[...])
        kn = _qk_norm(k_ref[...], gk_ref[...])
        v = v_ref[...]
        kc = c_ref[0, pl.ds(0, rows, stride=2), :]
        vc = c_ref[0, pl.ds(1, rows, stride=2), :]
        s_o = _dot(qn, kc, NT) * scale
        s_n = _dot(qn, kn, NT) * scale
        d_o = rows + lax.broadcasted_iota(jnp.int32, (ts, rows), 0) - lax.broadcasted_iota(jnp.int32, (ts, rows), 1)
        d_n = lax.broadcasted_iota(jnp.int32, (ts, ts), 0) - lax.broadcasted_iota(jnp.int32, (ts, ts), 1)
        ok_o = (d_o <= win) & ((d_o % dil) == 0)
        ok_n = (d_n >= 0) & ((d_n % dil) == 0)
        s_o = jnp.where(ok_o, s_o, neg)
        s_n = jnp.where(ok_n, s_n, neg)
        m = jnp.maximum(jnp.max(s_o, axis=-1, keepdims=True), jnp.max(s_n, axis=-1, keepdims=True))
        p_o = jnp.exp(s_o - m)
        p_n = jnp.exp(s_n - m)
        l = jnp.sum(p_o, axis=-1, keepdims=True) + jnp.sum(p_n, axis=-1, keepdims=True)
        outs.append((_dot(p_o, vc) + _dot(p_n, v)) / l)
        lses.append(m + jnp.log(l))
        n_ref[0, 0:2 * (rows - ts), :] = c_ref[0, 2 * ts:2 * rows, :]
        n_ref[0, pl.ds(2 * (rows - ts), ts, stride=2), :] = kn
        n_ref[0, pl.ds(2 * (rows - ts) + 1, ts, stride=2), :] = v
    m = jnp.maximum(jnp.maximum(lses[0], lses[1]), lses[2])
    es = [jnp.exp(x - m) for x in lses]
    o = (es[0] * outs[0] + es[1] * outs[1] + es[2] * outs[2]) / (es[0] + es[1] + es[2])
    o_ref[...] = o.astype(o_ref.dtype)


def _dswa_sample(u, row0, caches, g_q, g_k, *, bsz, ts):
    hd = B_HD
    ng = len(B_GROUPS)
    gh = caches[0].shape[3]
    nh = ng * gh
    rb0 = row0 // ts
    in_specs = []
    for g in range(ng):
        for part in range(3):
            in_specs.append(pl.BlockSpec((ts, hd), lambda bi, h, g=g, part=part: (rb0 + bi, part * nh + g * gh + h)))
    cache_args, c_specs, c_shapes = [], [], []
    for cch in caches:
        rows = cch.shape[1]
        cache_args.append(cch.reshape(bsz, 2 * rows, gh * hd))
        c_specs.append(pl.BlockSpec((1, 2 * rows, hd), lambda bi, h: (bi, 0, h)))
        c_shapes.append(jax.ShapeDtypeStruct((bsz, 2 * rows, gh * hd), F32))
    in_specs += c_specs + [pl.BlockSpec((1, hd), lambda bi, h: (0, 0))] * 2
    res = pl.pallas_call(
        _dswa_sample_kernel,
        grid=(bsz, gh),
        in_specs=in_specs,
        out_specs=[pl.BlockSpec((ts, hd), lambda bi, h: (bi, h))] + c_specs,
        out_shape=[jax.ShapeDtypeStruct((bsz * ts, gh * hd), F32)] + c_shapes,
        compiler_params=_cp("parallel", "parallel"),
        name="dswa_sample",
    )(*([u] * 9), *cache_args, g_q.reshape(1, hd), g_k.reshape(1, hd))
    return res[0], [b.reshape(cch.shape) for b, cch in zip(res[1:], caches)]


def kernel(x_prompt, x_sample, state_wkv, state_shift, cache_swa_kv1, cache_swa_kv2, cache_swa_kv3, cache_mem_kv, mem_prompt, norm_mix, norm_ffn, norm_mem, w_mem_kv, q_norm_mem, k_norm_mem, w_in_a, w_out_a, mu_a, w0_a, w2_a, a0_a, a2_a, g2_a, kk_a, ka_a, rk_a, lnx_g_a, lnx_b_a, w_in_b, w_out_b, q_norm_b, k_norm_b, w_ffn_in, w_ffn_out):
    bp, t, d = x_prompt.shape
    bs, ts, _ = x_sample.shape
    n_mem = mem_prompt.shape[1]
    mem_w = w_mem_kv.shape[2] // 2
    mem_hd = mem_w // MEM_HEADS
    mix_w = w0_a.shape[1]
    depth = norm_mix.shape[0]
    swa_in = (cache_swa_kv1, cache_swa_kv2, cache_swa_kv3)
    mp_rows = bp * t
    ms_rows = bs * ts
    rows = mp_rows + ms_rows

    y = jnp.concatenate([x_prompt.reshape(mp_rows, d), x_sample.reshape(ms_rows, d)], axis=0)
    mem2d = mem_prompt.reshape(bp * n_mem, d)

    wkv_p, shift_p, wkv_s, shift_s, mem_p = [], [], [], [], []
    swa_p = [[] for _ in B_GROUPS]
    swa_s = [[] for _ in B_GROUPS]
    for i in range(depth):
        j = i // 2
        mkv_p = _mem_kv(mem2d, norm_mem[i], w_mem_kv, i, k_norm_mem[i], hd=mem_hd)
        mem_p.append(mkv_p.reshape(bp, n_mem, 2, MEM_HEADS, mem_hd))
        mkv_p = mkv_p.reshape(bp, n_mem, 2 * mem_w)
        mkv_s = cache_mem_kv[i].reshape(bs, n_mem, 2 * mem_w)
        h = _rmsnorm(y, norm_mix[i])
        if i % 2 == 0:
            rkv_w = 3 * mix_w
            a_shift_w = mu_a.shape[1]
            w_tail = jnp.concatenate([_lora_slot(w_in_a[j][:, rkv_w:a_shift_w]), w_in_a[j][:, a_shift_w:]], axis=1)
            u_main = _matmul_ws(h, w_in_a, j, n=rkv_w, name="w_in_a")
            u_tail = _matmul_ws(h, w_tail[None], 0, name="w_in_a_tail")
            pad_rows = lambda w: jnp.pad(w, ((0, LANES - w.shape[0]), (0, 0)))
            prm = (mu_a[j][:rkv_w], _lora_slot(mu_a[j][rkv_w:]), w0_a[j], a0_a[j], kk_a[j], ka_a[j],
                   rk_a[j].reshape(-1), lnx_g_a[j], lnx_b_a[j], pad_rows(w2_a[j]), pad_rows(a2_a[j]), g2_a[j])
            heads = mix_w // A_HD
            op, sp = _rwkv_mixer(u_main, u_tail, 0, jnp.zeros((bp, 1, rkv_w), F32), jnp.zeros((bp, 1, LORA_PAD), F32),
                                 jnp.zeros((bp, heads, A_HD, A_HD), F32), prm, bsz=bp, t=t, t_valid=CHUNK, hp=8,
                                 out_rows=rows, out_cols=mix_w + mem_w)
            padt = lambda z: jnp.pad(z[mp_rows:].reshape(bs, ts, -1), ((0, 0), (0, CHUNK - ts), (0, 0))).reshape(bs * CHUNK, -1)
            ms, ss = _rwkv_mixer(padt(u_main), padt(u_tail[:, :LORA_PAD]), 0, state_shift[j][:, :, :rkv_w],
                                 _lora_slot(state_shift[j][:, :, rkv_w:]), state_wkv[j], prm, bsz=bs, t=CHUNK,
                                 t_valid=ts, hp=8, out_rows=bs * CHUNK, out_cols=mix_w)
            ms = ms.reshape(bs, CHUNK, mix_w)[:, :ts].reshape(ms_rows, mix_w)
            op = _mem_attend(u_tail, 1, 0, mkv_p, q_norm_mem[i], bsz=bp, t=t, tq=512, into=op, out_col=mix_w // mem_w)
            as_ = _mem_attend(u_tail, 1, mp_rows, mkv_s, q_norm_mem[i], bsz=bs, t=ts, tq=ts)
            wkv_p.append(sp)
            wkv_s.append(ss)
            last = lambda z, b_, t_: z.reshape(b_, t_, -1)[:, -1:]
            shift_p.append(jnp.concatenate([last(u_main[:mp_rows], bp, t), _lora_unslot(last(u_tail[:mp_rows, :LORA_PAD], bp, t))], axis=-1))
            shift_s.append(jnp.concatenate([last(u_main[mp_rows:], bs, ts), _lora_unslot(last(u_tail[mp_rows:, :LORA_PAD], bs, ts))], axis=-1))
            w_out = w_out_a
        else:
            u = _matmul_ws(h, w_in_b, j, name="w_in_b")
            gh = cache_swa_kv1.shape[4]
            att_w = gh * B_HD
            op, bufs_p = _dswa_prompt(u, q_norm_b[j], k_norm_b[j], bsz=bp, t=t, gh=gh, out_rows=rows, out_cols=att_w + mem_w)
            ms, bufs_s = _dswa_sample(u, mp_rows, [c[j] for c in swa_in], q_norm_b[j], k_norm_b[j], bsz=bs, ts=ts)
            mem_col = (3 * mix_w) // mem_w
            op = _mem_attend(u, mem_col, 0, mkv_p, q_norm_mem[i], bsz=bp, t=t, tq=512, into=op, out_col=att_w // mem_w)
            as_ = _mem_attend(u, mem_col, mp_rows, mkv_s, q_norm_mem[i], bsz=bs, t=ts, tq=ts)
            for g in range(len(B_GROUPS)):
                swa_p[g].append(bufs_p[g])
                swa_s[g].append(bufs_s[g])
            w_out = w_out_b
        op = lax.dynamic_update_slice(op, jnp.concatenate([ms.astype(BF16), as_.astype(BF16)], axis=1), (mp_rows, 0))
        y = _matmul_ws(op, w_out, j, res=y, name="w_out")
        y = _ffn(y, norm_ffn[i], w_ffn_in, w_ffn_out, i)
    return (y[:mp_rows].reshape(bp, t, d), y[mp_rows:].reshape(bs, ts, d),
            jnp.stack(wkv_p), jnp.stack(shift_p),
            jnp.stack(swa_p[0]), jnp.stack(swa_p[1]), jnp.stack(swa_p[2]),
            jnp.stack(mem_p),
            jnp.stack(wkv_s), jnp.stack(shift_s),
            jnp.stack(swa_s[0]), jnp.stack(swa_s[1]), jnp.stack(swa_s[2]))
```

```python
import functools

import jax
import jax.numpy as jnp
from jax import lax
from jax.experimental import pallas as pl
from jax.experimental.pallas import tpu as pltpu

F32 = jnp.float32
BF16 = jnp.bfloat16

RMS_EPS = 1e-6
LNX_EPS = 64e-5

LANES = 128
SUBLANES = 8
VMEM_LIMIT_BYTES = 56 * 2**20

MEM_HEADS = 4
A_HD = 64
B_HD = 128
B_GROUPS = ((128, 1), (512, 4), (2048, 16))
DECAY_LORA = 96
AAA_LORA = 96
GATE_LORA = 384
CHUNK = 64
LORA_PAD = 1024

NT = (((1,), (1,)), ((), ()))
TN = (((0,), (0,)), ((), ()))
NN = (((1,), (0,)), ((), ()))


def _cp(*sem):
    return pltpu.CompilerParams(dimension_semantics=sem, vmem_limit_bytes=VMEM_LIMIT_BYTES)


def _dot(a, b, dims=NN):
    return lax.dot_general(a.astype(BF16), b.astype(BF16), dims, preferred_element_type=F32)


def _sigmoid(x):
    return 1.0 / (1.0 + jnp.exp(-x))


def _rmsnorm_kernel(x_ref, g_ref, o_ref):
    x = x_ref[...]
    y = x * lax.rsqrt(jnp.mean(x * x, axis=-1, keepdims=True) + RMS_EPS)
    o_ref[...] = (y * g_ref[...]).astype(o_ref.dtype)


def _rmsnorm(x, g):
    m, d = x.shape
    tm = min(m, 512)
    return pl.pallas_call(
        _rmsnorm_kernel,
        grid=(m // tm,),
        in_specs=[pl.BlockSpec((tm, d), lambda i: (i, 0)), pl.BlockSpec((1, d), lambda i: (0, 0))],
        out_specs=pl.BlockSpec((tm, d), lambda i: (i, 0)),
        out_shape=jax.ShapeDtypeStruct((m, d), BF16),
        compiler_params=_cp("parallel"),
        name="rmsnorm",
    )(x, g.reshape(1, d))


def _mm2_kernel(*refs, has_s, has_res, cast, n_p, swiglu):
    it = iter(refs)
    xp = next(it)
    xs = next(it) if has_s else None
    w = [next(it), next(it)] if swiglu else [next(it)]
    rp = next(it) if has_res else None
    rs = next(it) if has_res and has_s else None
    op = next(it)
    os_ = next(it) if has_s else None
    wb = next(it) if cast else None
    i = pl.program_id(1)
    tn = w[0].shape[1]
    if cast:
        @pl.when(i == 0)
        def _():
            for c, wr in enumerate(w):
                wb[:, c * tn:(c + 1) * tn] = wr[...].astype(BF16)
        wmat = wb
    else:
        wmat = w[0]

    def tile(x_ref, r_ref, o_ref):
        acc = jnp.dot(x_ref[...], wmat[...], preferred_element_type=F32)
        if swiglu:
            g = acc[:, 0:tn]
            acc = g * _sigmoid(g) * acc[:, tn:2 * tn]
        if r_ref is not None:
            acc = acc + r_ref[...]
        o_ref[...] = acc.astype(o_ref.dtype)

    if has_s:
        @pl.when(i < n_p)
        def _():
            tile(xp, rp, op)

        @pl.when(i == n_p)
        def _():
            tile(xs, rs, os_)
    else:
        tile(xp, rp, op)


def _mm2(xp, xs, w, layer, *, n=None, tm, tn, res=None, cast=True, swiglu=False, out_dtype=F32, name="matmul"):
    mp, k = xp.shape
    width = w.shape[2] // 2 if swiglu else w.shape[2]
    n = width if n is None else n
    nj = n // tn
    n_p = mp // tm
    has_s = xs is not None
    has_res = res is not None
    prow = lambda j, i: (jnp.minimum(i, n_p - 1), 0)
    pout = lambda j, i: (jnp.minimum(i, n_p - 1), j)
    wspec = lambda off: pl.BlockSpec((pl.Squeezed(), k, tn), lambda j, i: (layer, 0, off + j))
    in_specs = [pl.BlockSpec((tm, k), prow)]
    args = [xp]
    if has_s:
        ms = xs.shape[0]
        in_specs.append(pl.BlockSpec((ms, k), lambda j, i: (0, 0)))
        args.append(xs)
    in_specs.append(wspec(0))
    args.append(w)
    if swiglu:
        in_specs.append(wspec(width // tn))
        args.append(w)
    if has_res:
        in_specs.append(pl.BlockSpec((tm, tn), pout))
        args.append(res[0])
        if has_s:
            in_specs.append(pl.BlockSpec((ms, tn), lambda j, i: (0, j)))
            args.append(res[1])
    out_specs = [pl.BlockSpec((tm, tn), pout)]
    out_shape = [jax.ShapeDtypeStruct((mp, n), out_dtype)]
    if has_s:
        out_specs.append(pl.BlockSpec((ms, tn), lambda j, i: (0, j)))
        out_shape.append(jax.ShapeDtypeStruct((ms, n), out_dtype))
    scratch = [pltpu.VMEM((k, (2 if swiglu else 1) * tn), BF16)] if cast else []
    out = pl.pallas_call(
        functools.partial(_mm2_kernel, has_s=has_s, has_res=has_res, cast=cast, n_p=n_p, swiglu=swiglu),
        grid=(nj, n_p + (1 if has_s else 0)),
        in_specs=in_specs,
        out_specs=out_specs,
        out_shape=out_shape,
        scratch_shapes=scratch,
        compiler_params=_cp("arbitrary", "arbitrary"),
        name=name,
    )(*args)
    return (out[0], out[1]) if has_s else (out[0], None)


def _ffn(yp, ys, g, w_in, w_out, layer):
    act_p, act_s = _mm2(_rmsnorm(yp, g), _rmsnorm(ys, g), w_in, layer, tm=1024, tn=256, swiglu=True,
                        out_dtype=BF16, name="ffn_in")
    return _mm2(act_p, act_s, w_out[layer].astype(BF16)[None], 0, tm=512, tn=512, res=(yp, ys), cast=False,
                name="ffn_out")


def _memkv_kernel(x_ref, g_ref, o_ref, *, heads, hd):
    w = heads * hd
    for h in range(heads):
        xh = x_ref[:, h * hd:(h + 1) * hd]
        o_ref[:, h * hd:(h + 1) * hd] = xh * lax.rsqrt(jnp.mean(xh * xh, axis=-1, keepdims=True) + RMS_EPS) * g_ref[...]
    o_ref[:, w:] = x_ref[:, w:]


def _mem_kv(mem2d, g_norm, w_kv, layer, g_k, *, hd):
    m = mem2d.shape[0]
    kv, _ = _mm2(_rmsnorm(mem2d, g_norm), None, w_kv, layer, tm=m, tn=512, name="mem_kv")
    n = kv.shape[1]
    tm = 256
    return pl.pallas_call(
        functools.partial(_memkv_kernel, heads=MEM_HEADS, hd=hd),
        grid=(m // tm,),
        in_specs=[pl.BlockSpec((tm, n), lambda i: (i, 0)), pl.BlockSpec((1, hd), lambda i: (0, 0))],
        out_specs=pl.BlockSpec((tm, n), lambda i: (i, 0)),
        out_shape=jax.ShapeDtypeStruct((m, n), F32),
        compiler_params=_cp("parallel"),
        name="mem_kv_norm",
    )(kv, g_k.reshape(1, hd))


def _mem_attend_kernel(*refs, heads, hd):
    q_ref, kv_ref, g_ref = refs[:3]
    o_ref = refs[-1]
    w = heads * hd
    for h in range(heads):
        q = q_ref[:, h * hd:(h + 1) * hd]
        qn = q * lax.rsqrt(jnp.mean(q * q, axis=-1, keepdims=True) + RMS_EPS) * g_ref[...]
        k = kv_ref[0, :, h * hd:(h + 1) * hd]
        v = kv_ref[0, :, w + h * hd:w + (h + 1) * hd]
        s = _dot(qn, k, NT) * (hd ** -0.5)
        p = jnp.exp(s - jnp.max(s, axis=-1, keepdims=True))
        o = _dot(p, v) / jnp.sum(p, axis=-1, keepdims=True)
        o_ref[:, h * hd:(h + 1) * hd] = o.astype(o_ref.dtype)


def _mem_attend(u, q_col, kv, g_q, *, bsz, t, tq, into=None, out_col=0):
    n_mem, w2 = kv.shape[1:]
    w = w2 // 2
    hd = w // MEM_HEADS
    tpb = t // tq
    in_specs = [pl.BlockSpec((tq, w), lambda i: (i, q_col)),
                pl.BlockSpec((1, n_mem, w2), lambda i: (i // tpb, 0, 0)),
                pl.BlockSpec((1, hd), lambda i: (0, 0))]
    args = [u, kv, g_q.reshape(1, hd)]
    if into is None:
        out_shape = jax.ShapeDtypeStruct((bsz * t, w), F32)
        aliases = {}
    else:
        in_specs.append(pl.BlockSpec(memory_space=pl.ANY))
        args.append(into)
        out_shape = jax.ShapeDtypeStruct(into.shape, into.dtype)
        aliases = {3: 0}
    return pl.pallas_call(
        functools.partial(_mem_attend_kernel, heads=MEM_HEADS, hd=hd),
        grid=(bsz * t // tq,),
        in_specs=in_specs,
        out_specs=pl.BlockSpec((tq, w), lambda i: (i, out_col)),
        out_shape=out_shape,
        input_output_aliases=aliases,
        compiler_params=_cp("parallel"),
        name="mem_attend",
    )(*args)


def _cumsum(x, tril_incl_bf16):
    hi = x.astype(BF16)
    lo = (x - hi.astype(F32)).astype(BF16)
    return (lax.dot_general(tril_incl_bf16, hi, NN, preferred_element_type=F32)
            + lax.dot_general(tril_incl_bf16, lo, NN, preferred_element_type=F32))


def _rwkv_kernel(ur, uk, uv, ul, zr, zk, zv, zl, mur, muk, muv, mul, w0, a0, kkp, kap, rkp, lg, lb, w2, a2, g2, s0_ref,
                 o_ref, st_ref, s_scr, pr_scr, pk_scr, pv_scr, pl_scr, *, hp, t_valid):
    c = pl.program_id(2)
    n_chunks = pl.num_programs(2)
    ln = ur.shape[0]
    l2 = 2 * ln
    hd = A_HD
    pairs = range(hp)

    @pl.when(c == 0)
    def _():
        s_scr[...] = s0_ref[0]
        pr_scr[...] = zr[0]
        pk_scr[...] = zk[0]
        pv_scr[...] = zv[0]
        pl_scr[...] = zl[0]

    def shift(x_ref, p_scr, mu_ref):
        x = x_ref[...]
        xp = pltpu.roll(x, 1, axis=0)
        row = lax.broadcasted_iota(jnp.int32, x.shape, 0)
        xp = jnp.where(row == 0, p_scr[...], xp)
        p_scr[...] = x[ln - 1:ln, :]
        return x + (xp - x) * mu_ref[...]

    r = shift(ur, pr_scr, mur)
    k = shift(uk, pk_scr, muk)
    v = shift(uv, pv_scr, muv)
    lo = shift(ul, pl_scr, mul)
    wl = lo[:, 0:LANES]
    al = lo[:, LANES:2 * LANES]
    gl = lo[:, 2 * LANES:2 * LANES + GATE_LORA]

    z = -(w0[...] + _dot(jnp.tanh(wl), w2[...]))
    lw = -jnp.exp(-(jnp.maximum(z, 0.0) + jnp.log(1.0 + jnp.exp(-jnp.abs(z)))) - 0.5)
    asig = _sigmoid(a0[...] + _dot(al, a2[...]))
    gate = _dot(_sigmoid(gl), g2[...])
    kk = k * kkp[...]
    k = k * (1.0 + (asig - 1.0) * kap[...])
    if t_valid < ln:
        valid = lax.broadcasted_iota(jnp.int32, (ln, 1), 0) < t_valid
        lw = jnp.where(valid, lw, 0.0)
        kk = jnp.where(valid, kk, 0.0)
        k = jnp.where(valid, k, 0.0)
        v = jnp.where(valid, v, 0.0)

    lane = lax.broadcasted_iota(jnp.int32, (1, LANES), 1)
    m0 = (lane < hd).astype(F32)
    m1 = 1.0 - m0
    row = lax.broadcasted_iota(jnp.int32, (ln, ln), 0)
    col = lax.broadcasted_iota(jnp.int32, (ln, ln), 1)
    tril_incl = (col <= row).astype(BF16)
    row2 = lax.broadcasted_iota(jnp.int32, (l2, l2), 0)
    col2 = lax.broadcasted_iota(jnp.int32, (l2, l2), 1)
    same = (row2 // ln) == (col2 // ln)
    strict2 = same & (col2 < row2)
    incl2 = same & (col2 <= row2)
    eye2 = (row2 == col2).astype(F32)
    rowk = lax.broadcasted_iota(jnp.int32, (LANES, LANES), 0)
    colk = lax.broadcasted_iota(jnp.int32, (LANES, LANES), 1)
    eyek = (rowk == colk).astype(F32)

    def sl(p):
        return slice(p * LANES, (p + 1) * LANES)

    def hsum(x):
        return (jnp.sum(x * m0, axis=-1, keepdims=True) * m0 + jnp.sum(x * m1, axis=-1, keepdims=True) * m1)

    def stack(x):
        return jnp.concatenate([x * m0, x * m1], axis=0)

    def unstack(xs):
        return xs[0:ln] + xs[ln:l2]

    r_ = [r[:, sl(p)] for p in pairs]
    k_ = [k[:, sl(p)] for p in pairs]
    v_ = [v[:, sl(p)] for p in pairs]
    lw_ = [lw[:, sl(p)] for p in pairs]
    kk_ = [kk[:, sl(p)] for p in pairs]
    kkn = [x / jnp.maximum(jnp.sqrt(hsum(x * x)), 1e-12) for x in kk_]
    a_ = [-x for x in kkn]
    b_ = [kkn[p] * asig[:, sl(p)] for p in pairs]

    cin = [_cumsum(x, tril_incl) for x in lw_]
    cl = [x[ln - 1:ln, :] for x in cin]
    inv = [jnp.exp(-x) for x in cin]
    dec = [jnp.exp(cl[p] - cin[p]) for p in pairs]
    a_s = [stack(a_[p] * jnp.exp(cin[p] - lw_[p])) for p in pairs]
    r_s = [stack(r_[p] * jnp.exp(cin[p])) for p in pairs]
    b_s = [stack(b_[p] * inv[p]) for p in pairs]
    k_s = [stack(k_[p] * inv[p]) for p in pairs]
    v_s = [stack(x) for x in v_]
    bh_s = [stack(b_[p] * dec[p]) for p in pairs]
    kh_s = [stack(k_[p] * dec[p]) for p in pairs]

    g = [_dot(jnp.concatenate([a_s[p], r_s[p]], axis=0), jnp.concatenate([b_s[p], k_s[p]], axis=0), NT) for p in pairs]
    a_ab = [jnp.where(strict2, x[0:l2, 0:l2], 0.0) for x in g]
    a_ak = [jnp.where(strict2, x[0:l2, l2:2 * l2], 0.0) for x in g]
    r_b = [jnp.where(incl2, x[l2:2 * l2, 0:l2], 0.0) for x in g]
    r_k = [jnp.where(incl2, x[l2:2 * l2, l2:2 * l2], 0.0) for x in g]

    pw = a_ab
    tinv = [eye2 + x for x in a_ab]
    for _ in range(ln.bit_length() - 2):
        pw = [_dot(x, x) for x in pw]
        tinv = [tinv[p] + _dot(tinv[p], pw[p]) for p in pairs]

    av_s = [_dot(a_ak[p], v_s[p]) for p in pairs]
    wz = [_dot(tinv[p], jnp.concatenate([a_s[p], av_s[p]], axis=1)) for p in pairs]
    w_s = [x[:, 0:LANES] for x in wz]
    z_s = [x[:, LANES:2 * LANES] for x in wz]
    zv_s = [jnp.concatenate([z_s[p], v_s[p]], axis=0) for p in pairs]
    q = [unstack(r_s[p] + _dot(r_b[p], w_s[p])) for p in pairs]
    y_loc = [unstack(_dot(jnp.concatenate([r_b[p], r_k[p]], axis=1), zv_s[p])) for p in pairs]
    mc = [eyek * jnp.exp(cl[p]) + _dot(w_s[p], bh_s[p], TN) for p in pairs]
    nc = [_dot(zv_s[p], jnp.concatenate([bh_s[p], kh_s[p]], axis=0), TN) for p in pairs]

    s = [s_scr[p] for p in pairs]
    y = [_dot(q[p], s[p], NT) + y_loc[p] for p in pairs]
    s_new = [_dot(s[p], mc[p]) + nc[p] for p in pairs]
    for p in pairs:
        s_scr[p] = s_new[p]

    inv_hd = 1.0 / hd
    mean = [hsum(x) * inv_hd for x in y]
    d = [y[p] - mean[p] for p in pairs]
    var = [hsum(x * x) * inv_hd for x in d]
    bonus = [hsum(r_[p] * k_[p] * rkp[:, sl(p)]) * v_[p] for p in pairs]
    for p in pairs:
        yn = d[p] * lax.rsqrt(var[p] + LNX_EPS) * lg[:, sl(p)] + lb[:, sl(p)]
        o_ref[:, sl(p)] = ((yn + bonus[p]) * gate[:, sl(p)]).astype(o_ref.dtype)

    @pl.when(c == n_chunks - 1)
    def _():
        st_ref[0] = s_scr[...]


def _lora_slot(x):
    z = lambda n: jnp.zeros(x.shape[:-1] + (n,), x.dtype)
    d, a = DECAY_LORA, AAA_LORA
    return jnp.concatenate([x[..., :d], z(LANES - d), x[..., d:d + a], z(LANES - a), x[..., d + a:],
                            z(LORA_PAD - 2 * LANES - GATE_LORA)], axis=-1)


def _lora_unslot(x):
    return jnp.concatenate([x[..., :DECAY_LORA], x[..., LANES:LANES + AAA_LORA],
                            x[..., 2 * LANES:2 * LANES + GATE_LORA]], axis=-1)


def _rwkv_mixer(u_main, u_tail, prev_main, prev_lora, s0, prm, *, bsz, t, t_valid, hp, out_cols):
    mu_main, mu_lora, w0, a0, kk, ka, rk, lnx_g, lnx_b, w2p, a2p, g2 = prm
    c = w0.shape[0]
    heads = c // A_HD
    n_pairs = heads // 2
    gw = hp * LANES
    ng = c // gw
    nch = t // CHUNK
    s0p = s0.reshape(bsz, n_pairs, 2, A_HD, A_HD)
    zero = jnp.zeros_like(s0p[:, :, 0])
    s0_bd = jnp.concatenate([jnp.concatenate([s0p[:, :, 0], zero], axis=-1),
                             jnp.concatenate([zero, s0p[:, :, 1]], axis=-1)], axis=-2)
    row1 = lambda a: a.reshape(1, -1)
    ublk = lambda off: pl.BlockSpec((CHUNK, gw), lambda bi, gi, ci: (bi * nch + ci, off * ng + gi))
    zblk = lambda off: pl.BlockSpec((1, 1, gw), lambda bi, gi, ci: (bi, 0, off * ng + gi))
    pblk = lambda off: pl.BlockSpec((1, gw), lambda bi, gi, ci: (0, off * ng + gi))
    lora_c = pl.BlockSpec((1, LORA_PAD), lambda bi, gi, ci: (0, 0))
    st_blk = pl.BlockSpec((1, hp, LANES, LANES), lambda bi, gi, ci: (bi, gi, 0, 0))
    in_specs = [
        ublk(0), ublk(1), ublk(2),
        pl.BlockSpec((CHUNK, LORA_PAD), lambda bi, gi, ci: (bi * nch + ci, 0)),
        zblk(0), zblk(1), zblk(2),
        pl.BlockSpec((1, 1, LORA_PAD), lambda bi, gi, ci: (bi, 0, 0)),
        pblk(0), pblk(1), pblk(2), lora_c,
        pblk(0), pblk(0), pblk(0), pblk(0), pblk(0), pblk(0), pblk(0),
        pl.BlockSpec((LANES, gw), lambda bi, gi, ci: (0, gi)),
        pl.BlockSpec((LANES, gw), lambda bi, gi, ci: (0, gi)),
        pl.BlockSpec((GATE_LORA, gw), lambda bi, gi, ci: (0, gi)),
        st_blk,
    ]
    out, st_bd = pl.pallas_call(
        functools.partial(_rwkv_kernel, hp=hp, t_valid=t_valid),
        grid=(bsz, ng, nch),
        in_specs=in_specs,
        out_specs=[pl.BlockSpec((CHUNK, gw), lambda bi, gi, ci: (bi * nch + ci, gi)), st_blk],
        out_shape=[jax.ShapeDtypeStruct((bsz * t, out_cols), BF16),
                   jax.ShapeDtypeStruct((bsz, n_pairs, LANES, LANES), F32)],
        scratch_shapes=[pltpu.VMEM((hp, LANES, LANES), F32), pltpu.VMEM((1, gw), F32), pltpu.VMEM((1, gw), F32),
                        pltpu.VMEM((1, gw), F32), pltpu.VMEM((1, LORA_PAD), F32)],
        compiler_params=_cp("parallel", "parallel", "arbitrary"),
        name="rwkv_mixer",
    )(u_main, u_main, u_main, u_tail, prev_main, prev_main, prev_main, prev_lora,
      row1(mu_main), row1(mu_main), row1(mu_main), row1(mu_lora),
      row1(w0), row1(a0), row1(kk), row1(ka), row1(rk), row1(lnx_g), row1(lnx_b), w2p, a2p, g2, s0_bd)
    st = jnp.stack([st_bd[:, :, :A_HD, :A_HD], st_bd[:, :, A_HD:, A_HD:]], axis=2)
    return out, st.reshape(bsz, heads, A_HD, A_HD)


def _qk_norm(x, g):
    return x * lax.rsqrt(jnp.mean(x * x, axis=-1, keepdims=True) + RMS_EPS) * g


DSWA_NB = 4


def _dswa_prompt_kernel(q0, k0, v0, q1, k1, v1, q2, k2, v2, gq_ref, gk_ref,
                        o_ref, c0, c1, c2, qn_scr, kn_scr, og_scr, lse_scr):
    hd = q0.shape[1]
    t = q0.shape[0]
    scale = hd ** -0.5
    neg = -1e30
    groups = ((q0, k0, v0, c0), (q1, k1, v1, c1), (q2, k2, v2, c2))
    for g, ((q_ref, k_ref, v_ref, c_ref), (win, dil)) in enumerate(zip(groups, B_GROUPS)):
        span = win // dil
        nb = (t // dil) // span
        keep = c_ref.shape[1] // 2
        qn_scr[...] = _qk_norm(q_ref[...], gq_ref[...])
        kn_scr[...] = _qk_norm(k_ref[...], gk_ref[...])
        c_ref[0, pl.ds(0, keep, stride=2), :] = kn_scr[t - keep:t, :]
        c_ref[0, pl.ds(1, keep, stride=2), :] = v_ref[t - keep:t, :]
        row = lax.broadcasted_iota(jnp.int32, (span, span), 0)
        col = lax.broadcasted_iota(jnp.int32, (span, span), 1)
        own_ok = col <= row
        prev_ok = col >= row
        per_it = min(DSWA_NB, nb)
        n_res = DSWA_NB // per_it
        its_per_res = nb // per_it

        def rows_at(start, dil=dil, span=span):
            return pl.ds(start, span) if dil == 1 else pl.ds(start, span, stride=dil)

        def body(it, carry, dil=dil, span=span, g=g, v_ref=v_ref, per_it=per_it, n_res=n_res,
                 its_per_res=its_per_res, own_ok=own_ok, prev_ok=prev_ok, rows_at=rows_at):
            blocks = []
            for rr in range(n_res):
                res = (it // its_per_res) * n_res + rr
                blk0 = (it % its_per_res) * per_it
                for jb in range(per_it):
                    start = res + dil * span * (blk0 + jb)
                    if jb > 0:
                        blocks.append((start, "shared", None))
                    elif its_per_res > 1:
                        blocks.append((start, jnp.maximum(start - dil * span, res), blk0 > 0))
                    else:
                        blocks.append((start, None, None))
            n = len(blocks)
            cur = [rows_at(bk[0]) for bk in blocks]
            qb = [qn_scr[c, :].astype(BF16) for c in cur]
            kc = [kn_scr[c, :].astype(BF16) for c in cur]
            vc = [v_ref[c, :].astype(BF16) for c in cur]
            kp, vp, pmask = [], [], []
            for j, (start, prv, has_prev) in enumerate(blocks):
                if prv is None:
                    kp.append(None)
                    vp.append(None)
                    pmask.append(None)
                elif isinstance(prv, str):
                    kp.append(kc[j - 1])
                    vp.append(vc[j - 1])
                    pmask.append(prev_ok)
                else:
                    rws = rows_at(prv)
                    kp.append(kn_scr[rws, :].astype(BF16))
                    vp.append(v_ref[rws, :].astype(BF16))
                    pmask.append(prev_ok & has_prev)
            s_c = [jnp.where(own_ok, _dot(qb[j], kc[j], NT) * scale, neg) for j in range(n)]
            s_p = [None if kp[j] is None else jnp.where(pmask[j], _dot(qb[j], kp[j], NT) * scale, neg)
                   for j in range(n)]
            m = [jnp.max(s_c[j], axis=-1, keepdims=True) if s_p[j] is None else
                 jnp.maximum(jnp.max(s_c[j], axis=-1, keepdims=True), jnp.max(s_p[j], axis=-1, keepdims=True))
                 for j in range(n)]
            p_c = [jnp.exp(s_c[j] - m[j]) for j in range(n)]
            p_p = [None if s_p[j] is None else jnp.exp(s_p[j] - m[j]) for j in range(n)]
            l = [jnp.sum(p_c[j], axis=-1, keepdims=True) if p_p[j] is None else
                 jnp.sum(p_c[j], axis=-1, keepdims=True) + jnp.sum(p_p[j], axis=-1, keepdims=True)
                 for j in range(n)]
            o = [_dot(p_c[j], vc[j]) if p_p[j] is None else _dot(p_c[j], vc[j]) + _dot(p_p[j], vp[j])
                 for j in range(n)]
            for j in range(n):
                og_scr[g, cur[j], :] = o[j] / l[j]
                lse_scr[g, cur[j], :] = jnp.broadcast_to(m[j] + jnp.log(l[j]), (span, hd))
            return carry

        lax.fori_loop(0, (dil * nb) // DSWA_NB, body, 0)

    l0, l1, l2 = lse_scr[0], lse_scr[1], lse_scr[2]
    m = jnp.maximum(jnp.maximum(l0, l1), l2)
    e0, e1, e2 = jnp.exp(l0 - m), jnp.exp(l1 - m), jnp.exp(l2 - m)
    o = (e0 * og_scr[0] + e1 * og_scr[1] + e2 * og_scr[2]) / (e0 + e1 + e2)
    o_ref[...] = o.astype(o_ref.dtype)


def _dswa_prompt(u, g_q, g_k, *, bsz, t, gh, out_cols):
    hd = B_HD
    ng = len(B_GROUPS)
    nh = ng * gh
    in_specs = []
    for g in range(ng):
        for part in range(3):
            in_specs.append(pl.BlockSpec((t, hd), lambda bi, h, g=g, part=part: (bi, part * nh + g * gh + h)))
    in_specs += [pl.BlockSpec((1, hd), lambda bi, h: (0, 0))] * 2
    keeps = [min(win, t) for win, _ in B_GROUPS]
    res = pl.pallas_call(
        _dswa_prompt_kernel,
        grid=(bsz, gh),
        in_specs=in_specs,
        out_specs=[pl.BlockSpec((t, hd), lambda bi, h: (bi, h))]
        + [pl.BlockSpec((1, 2 * kp, hd), lambda bi, h: (bi, 0, h)) for kp in keeps],
        out_shape=[jax.ShapeDtypeStruct((bsz * t, out_cols), BF16)]
        + [jax.ShapeDtypeStruct((bsz, 2 * kp, gh * hd), F32) for kp in keeps],
        scratch_shapes=[pltpu.VMEM((t, hd), F32), pltpu.VMEM((t, hd), F32),
                        pltpu.VMEM((ng, t, hd), F32), pltpu.VMEM((ng, t, hd), F32)],
        compiler_params=_cp("parallel", "parallel"),
        name="dswa_prompt",
    )(*([u] * 9), g_q.reshape(1, hd), g_k.reshape(1, hd))
    return res[0], [b.reshape(bsz, kp, 2, gh, hd) for b, kp in zip(res[1:], keeps)]


def _dswa_sample_kernel(q0, k0, v0, q1, k1, v1, q2, k2, v2, c0, c1, c2, gq_ref, gk_ref, o_ref, n0, n1, n2):
    hd = q0.shape[1]
    ts = q0.shape[0]
    scale = hd ** -0.5
    neg = -1e30
    groups = ((q0, k0, v0, c0, n0), (q1, k1, v1, c1, n1), (q2, k2, v2, c2, n2))
    outs, lses = [], []
    for (q_ref, k_ref, v_ref, c_ref, n_ref), (win, dil) in zip(groups, B_GROUPS):
        rows = c_ref.shape[1] // 2
        qn = _qk_norm(q_ref[...], gq_ref[...])
        kn = _qk_norm(k_ref[...], gk_ref[...])
        v = v_ref[...]
        kc = c_ref[0, pl.ds(0, rows, stride=2), :]
        vc = c_ref[0, pl.ds(1, rows, stride=2), :]
        s_o = _dot(qn, kc, NT) * scale
        s_n = _dot(qn, kn, NT) * scale
        d_o = rows + lax.broadcasted_iota(jnp.int32, (ts, rows), 0) - lax.broadcasted_iota(jnp.int32, (ts, rows), 1)
        d_n = lax.broadcasted_iota(jnp.int32, (ts, ts), 0) - lax.broadcasted_iota(jnp.int32, (ts, ts), 1)
        ok_o = (d_o <= win) & ((d_o % dil) == 0)
        ok_n = (d_n >= 0) & ((d_n % dil) == 0)
        s_o = jnp.where(ok_o, s_o, neg)
        s_n = jnp.where(ok_n, s_n, neg)
        m = jnp.maximum(jnp.max(s_o, axis=-1, keepdims=True), jnp.max(s_n, axis=-1, keepdims=True))
        p_o = jnp.exp(s_o - m)
        p_n = jnp.exp(s_n - m)
        l = jnp.sum(p_o, axis=-1, keepdims=True) + jnp.sum(p_n, axis=-1, keepdims=True)
        outs.append((_dot(p_o, vc) + _dot(p_n, v)) / l)
        lses.append(m + jnp.log(l))
        n_ref[0, 0:2 * (rows - ts), :] = c_ref[0, 2 * ts:2 * rows, :]
        n_ref[0, pl.ds(2 * (rows - ts), ts, stride=2), :] = kn
        n_ref[0, pl.ds(2 * (rows - ts) + 1, ts, stride=2), :] = v
    m = jnp.maximum(jnp.maximum(lses[0], lses[1]), lses[2])
    es = [jnp.exp(x - m) for x in lses]
    o = (es[0] * outs[0] + es[1] * outs[1] + es[2] * outs[2]) / (es[0] + es[1] + es[2])
    o_ref[...] = o.astype(o_ref.dtype)


def _dswa_sample(u, caches, g_q, g_k, *, bsz, ts):
    hd = B_HD
    ng = len(B_GROUPS)
    gh = caches[0].shape[3]
    nh = ng * gh
    in_specs = []
    for g in range(ng):
        for part in range(3):
            in_specs.append(pl.BlockSpec((ts, hd), lambda bi, h, g=g, part=part: (bi, part * nh + g * gh + h)))
    cache_args, c_specs, c_shapes = [], [], []
    for cch in caches:
        rows = cch.shape[1]
        cache_args.append(cch.reshape(bsz, 2 * rows, gh * hd))
        c_specs.append(pl.BlockSpec((1, 2 * rows, hd), lambda bi, h: (bi, 0, h)))
        c_shapes.append(jax.ShapeDtypeStruct((bsz, 2 * rows, gh * hd), F32))
    in_specs += c_specs + [pl.BlockSpec((1, hd), lambda bi, h: (0, 0))] * 2
    res = pl.pallas_call(
        _dswa_sample_kernel,
        grid=(bsz, gh),
        in_specs=in_specs,
        out_specs=[pl.BlockSpec((ts, hd), lambda bi, h: (bi, h))] + c_specs,
        out_shape=[jax.ShapeDtypeStruct((bsz * ts, gh * hd), F32)] + c_shapes,
        compiler_params=_cp("parallel", "parallel"),
        name="dswa_sample",
    )(*([u] * 9), *cache_args, g_q.reshape(1, hd), g_k.reshape(1, hd))
    return res[0], [b.reshape(cch.shape) for b, cch in zip(res[1:], caches)]


def kernel(x_prompt, x_sample, state_wkv, state_shift, cache_swa_kv1, cache_swa_kv2, cache_swa_kv3, cache_mem_kv, mem_prompt, norm_mix, norm_ffn, norm_mem, w_mem_kv, q_norm_mem, k_norm_mem, w_in_a, w_out_a, mu_a, w0_a, w2_a, a0_a, a2_a, g2_a, kk_a, ka_a, rk_a, lnx_g_a, lnx_b_a, w_in_b, w_out_b, q_norm_b, k_norm_b, w_ffn_in, w_ffn_out):
    bp, t, d = x_prompt.shape
    bs, ts, _ = x_sample.shape
    n_mem = mem_prompt.shape[1]
    mem_w = w_mem_kv.shape[2] // 2
    mem_hd = mem_w // MEM_HEADS
    mix_w = w0_a.shape[1]
    depth = norm_mix.shape[0]
    swa_in = (cache_swa_kv1, cache_swa_kv2, cache_swa_kv3)
    mp_rows = bp * t
    ms_rows = bs * ts
    tm = 1024

    yp = x_prompt.reshape(mp_rows, d)
    ys = x_sample.reshape(ms_rows, d)
    mem2d = mem_prompt.reshape(bp * n_mem, d)

    wkv_p, shift_p, wkv_s, shift_s, mem_p = [], [], [], [], []
    swa_p = [[] for _ in B_GROUPS]
    swa_s = [[] for _ in B_GROUPS]
    for i in range(depth):
        j = i // 2
        mkv_p = _mem_kv(mem2d, norm_mem[i], w_mem_kv, i, k_norm_mem[i], hd=mem_hd)
        mem_p.append(mkv_p.reshape(bp, n_mem, 2, MEM_HEADS, mem_hd))
        mkv_p = mkv_p.reshape(bp, n_mem, 2 * mem_w)
        mkv_s = cache_mem_kv[i].reshape(bs, n_mem, 2 * mem_w)
        hp_ = _rmsnorm(yp, norm_mix[i])
        hs_ = _rmsnorm(ys, norm_mix[i])
        if i % 2 == 0:
            rkv_w = 3 * mix_w
            a_shift_w = mu_a.shape[1]
            w_tail = jnp.concatenate([_lora_slot(w_in_a[j][:, rkv_w:a_shift_w]), w_in_a[j][:, a_shift_w:]], axis=1)
            um_p, um_s = _mm2(hp_, hs_, w_in_a, j, n=rkv_w, tm=tm, tn=512, name="w_in_a")
            ut_p, ut_s = _mm2(hp_, hs_, w_tail[None], 0, tm=tm, tn=512, name="w_in_a_tail")
            pad_rows = lambda w: jnp.pad(w, ((0, LANES - w.shape[0]), (0, 0)))
            prm = (mu_a[j][:rkv_w], _lora_slot(mu_a[j][rkv_w:]), w0_a[j], a0_a[j], kk_a[j], ka_a[j],
                   rk_a[j].reshape(-1), lnx_g_a[j], lnx_b_a[j], pad_rows(w2_a[j]), pad_rows(a2_a[j]), g2_a[j])
            heads = mix_w // A_HD
            op, sp = _rwkv_mixer(um_p, ut_p, jnp.zeros((bp, 1, rkv_w), F32), jnp.zeros((bp, 1, LORA_PAD), F32),
                                 jnp.zeros((bp, heads, A_HD, A_HD), F32), prm, bsz=bp, t=t, t_valid=CHUNK, hp=8,
                                 out_cols=mix_w + mem_w)
            padt = lambda z: jnp.pad(z.reshape(bs, ts, -1), ((0, 0), (0, CHUNK - ts), (0, 0))).reshape(bs * CHUNK, -1)
            ms, ss = _rwkv_mixer(padt(um_s), padt(ut_s[:, :LORA_PAD]), state_shift[j][:, :, :rkv_w],
                                 _lora_slot(state_shift[j][:, :, rkv_w:]), state_wkv[j], prm, bsz=bs, t=CHUNK,
                                 t_valid=ts, hp=8, out_cols=mix_w)
            ms = ms.reshape(bs, CHUNK, mix_w)[:, :ts].reshape(ms_rows, mix_w)
            op = _mem_attend(ut_p, 1, mkv_p, q_norm_mem[i], bsz=bp, t=t, tq=512, into=op, out_col=mix_w // mem_w)
            as_ = _mem_attend(ut_s, 1, mkv_s, q_norm_mem[i], bsz=bs, t=ts, tq=ts)
            wkv_p.append(sp)
            wkv_s.append(ss)
            last = lambda z, b_, t_: z.reshape(b_, t_, -1)[:, -1:]
            shift_p.append(jnp.concatenate([last(um_p, bp, t), _lora_unslot(last(ut_p[:, :LORA_PAD], bp, t))], axis=-1))
            shift_s.append(jnp.concatenate([last(um_s, bs, ts), _lora_unslot(last(ut_s[:, :LORA_PAD], bs, ts))], axis=-1))
            w_out = w_out_a
        else:
            u_p, u_s = _mm2(hp_, hs_, w_in_b, j, tm=tm, tn=512, name="w_in_b")
            gh = cache_swa_kv1.shape[4]
            att_w = gh * B_HD
            op, bufs_p = _dswa_prompt(u_p, q_norm_b[j], k_norm_b[j], bsz=bp, t=t, gh=gh, out_cols=att_w + mem_w)
            ms, bufs_s = _dswa_sample(u_s, [c[j] for c in swa_in], q_norm_b[j], k_norm_b[j], bsz=bs, ts=ts)
            mem_col = (3 * mix_w) // mem_w
            op = _mem_attend(u_p, mem_col, mkv_p, q_norm_mem[i], bsz=bp, t=t, tq=512, into=op, out_col=att_w // mem_w)
            as_ = _mem_attend(u_s, mem_col, mkv_s, q_norm_mem[i], bsz=bs, t=ts, tq=ts)
            for g in range(len(B_GROUPS)):
                swa_p[g].append(bufs_p[g])
                swa_s[g].append(bufs_s[g])
            w_out = w_out_b
        os_ = jnp.concatenate([ms.astype(BF16), as_.astype(BF16)], axis=1)
        yp, ys = _mm2(op, os_, w_out, j, tm=tm, tn=512, res=(yp, ys), name="w_out")
        yp, ys = _ffn(yp, ys, norm_ffn[i], w_ffn_in, w_ffn_out, i)
    return (yp.reshape(bp, t, d), ys.reshape(bs, ts, d),
            jnp.stack(wkv_p), jnp.stack(shift_p),
            jnp.stack(swa_p[0]), jnp.stack(swa_p[1]), jnp.stack(swa_p[2]),
            jnp.stack(mem_p),
            jnp.stack(wkv_s), jnp.stack(shift_s),
            jnp.stack(swa_s[0]), jnp.stack(swa_s[1]), jnp.stack(swa_s[2]))
```

```python
import functools

import jax
import jax.numpy as jnp
from jax import lax
from jax.experimental import pallas as pl
from jax.experimental.pallas import tpu as pltpu

F32 = jnp.float32
BF16 = jnp.bfloat16

RMS_EPS = 1e-6
LNX_EPS = 64e-5

LANES = 128
SUBLANES = 8
VMEM_LIMIT_BYTES = 56 * 2**20

MEM_HEADS = 4
A_HD = 64
B_HD = 128
B_GROUPS = ((128, 1), (512, 4), (2048, 16))
DECAY_LORA = 96
AAA_LORA = 96
GATE_LORA = 384
CHUNK = 64
LORA_PAD = 1024

NT = (((1,), (1,)), ((), ()))
TN = (((0,), (0,)), ((), ()))
NN = (((1,), (0,)), ((), ()))


def _cp(*sem):
    return pltpu.CompilerParams(dimension_semantics=sem, vmem_limit_bytes=VMEM_LIMIT_BYTES)


def _dot(a, b, dims=NN):
    return lax.dot_general(a.astype(BF16), b.astype(BF16), dims, preferred_element_type=F32)


def _sigmoid(x):
    return 1.0 / (1.0 + jnp.exp(-x))


def _rmsnorm_kernel(x_ref, g_ref, o_ref):
    x = x_ref[...]
    y = x * lax.rsqrt(jnp.mean(x * x, axis=-1, keepdims=True) + RMS_EPS)
    o_ref[...] = (y * g_ref[...]).astype(o_ref.dtype)


def _rmsnorm(x, g):
    m, d = x.shape
    tm = min(m, 512)
    return pl.pallas_call(
        _rmsnorm_kernel,
        grid=(m // tm,),
        in_specs=[pl.BlockSpec((tm, d), lambda i: (i, 0)), pl.BlockSpec((1, d), lambda i: (0, 0))],
        out_specs=pl.BlockSpec((tm, d), lambda i: (i, 0)),
        out_shape=jax.ShapeDtypeStruct((m, d), BF16),
        compiler_params=_cp("parallel"),
        name="rmsnorm",
    )(x, g.reshape(1, d))


SUB = 256


def _mm_panel_kernel(*refs, has_s, has_res, n_p, nsub, swiglu):
    it = iter(refs)
    xp = next(it)
    xs = next(it) if has_s else None
    w = next(it)
    rp = next(it) if has_res else None
    rs = next(it) if has_res and has_s else None
    op = next(it)
    os_ = next(it) if has_s else None
    wb = next(it)
    i = pl.program_id(1)
    nout = nsub // 2 if swiglu else nsub

    @pl.when(i < nsub)
    def _():
        wb[i] = w[...].astype(BF16)

    def tile(x_ref, r_ref, o_ref):
        x = x_ref[...]
        for s in range(nout):
            cols = slice(s * SUB, (s + 1) * SUB)
            acc = jnp.dot(x, wb[s], preferred_element_type=F32)
            if swiglu:
                acc = acc * _sigmoid(acc) * jnp.dot(x, wb[nout + s], preferred_element_type=F32)
            if r_ref is not None:
                acc = acc + r_ref[:, cols]
            o_ref[:, cols] = acc.astype(o_ref.dtype)

    @pl.when((i >= nsub) & (i < nsub + n_p))
    def _():
        tile(xp, rp, op)

    if has_s:
        @pl.when(i == nsub + n_p)
        def _():
            tile(xs, rs, os_)


def _mm_panel(xp, xs, w, layer, *, n=None, tm, spp, res=None, swiglu=False, out_dtype=F32, w_buffers=2,
              name="matmul"):
    mp, k = xp.shape
    width = w.shape[2] // 2 if swiglu else w.shape[2]
    n = width if n is None else n
    n_sub_total = n // SUB
    n_panels = -(-n_sub_total // spp)
    nsub = 2 * spp if swiglu else spp
    n_p = mp // tm
    pw = spp * SUB
    has_s = xs is not None
    has_res = res is not None
    up0 = width // SUB

    def w_idx(j, i):
        ii = jnp.minimum(i, nsub - 1)
        first = jnp.minimum(j * spp + jnp.minimum(ii, spp - 1), n_sub_total - 1)
        if not swiglu:
            return (layer, 0, first)
        second = up0 + jnp.minimum(j * spp + jnp.maximum(ii - spp, 0), n_sub_total - 1)
        return (layer, 0, jnp.where(ii < spp, first, second))

    prow = lambda j, i: (jnp.clip(i - nsub, 0, n_p - 1), 0)
    pout = lambda j, i: (jnp.clip(i - nsub, 0, n_p - 1), j)
    in_specs = [pl.BlockSpec((tm, k), prow)]
    args = [xp]
    if has_s:
        ms = xs.shape[0]
        in_specs.append(pl.BlockSpec((ms, k), lambda j, i: (0, 0)))
        args.append(xs)
    w_mode = {} if w_buffers == 2 else {"pipeline_mode": pl.Buffered(w_buffers)}
    in_specs.append(pl.BlockSpec((pl.Squeezed(), k, SUB), w_idx, **w_mode))
    args.append(w)
    if has_res:
        in_specs.append(pl.BlockSpec((tm, pw), pout))
        args.append(res[0])
        if has_s:
            in_specs.append(pl.BlockSpec((ms, pw), lambda j, i: (0, j)))
            args.append(res[1])
    out_specs = [pl.BlockSpec((tm, pw), pout)]
    out_shape = [jax.ShapeDtypeStruct((mp, n), out_dtype)]
    if has_s:
        out_specs.append(pl.BlockSpec((ms, pw), lambda j, i: (0, j)))
        out_shape.append(jax.ShapeDtypeStruct((ms, n), out_dtype))
    out = pl.pallas_call(
        functools.partial(_mm_panel_kernel, has_s=has_s, has_res=has_res, n_p=n_p, nsub=nsub, swiglu=swiglu),
        grid=(n_panels, nsub + n_p + (1 if has_s else 0)),
        in_specs=in_specs,
        out_specs=out_specs,
        out_shape=out_shape,
        scratch_shapes=[pltpu.VMEM((nsub, k, SUB), BF16)],
        compiler_params=_cp("arbitrary", "arbitrary"),
        name=name,
    )(*args)
    return (out[0], out[1]) if has_s else (out[0], None)


def _mm_panel2_kernel(*refs, has_s, has_res, n_p, nsub, n_panels, swiglu, w_t):
    wdims = NT if w_t else NN
    it = iter(refs)
    xp = next(it)
    xs = next(it) if has_s else None
    w = next(it)
    rp = next(it) if has_res else None
    rs = next(it) if has_res and has_s else None
    op = next(it)
    os_ = next(it) if has_s else None
    wbs = (next(it), next(it))
    j = pl.program_id(0)
    i = pl.program_id(1)
    nout = nsub // 2 if swiglu else nsub
    do_cast = (j < n_panels) & (i < nsub)
    do_mm = (j >= 1) & (i < n_p)

    def tile(x_ref, r_ref, o_ref, wb):
        x = x_ref[...]
        for s in range(nout):
            cols = slice(s * SUB, (s + 1) * SUB)
            acc = lax.dot_general(x, wb[s], wdims, preferred_element_type=F32)
            if swiglu:
                acc = acc * _sigmoid(acc) * lax.dot_general(x, wb[nout + s], wdims, preferred_element_type=F32)
            if r_ref is not None:
                acc = acc + r_ref[:, cols]
            o_ref[:, cols] = acc.astype(o_ref.dtype)

    for par in (0, 1):
        mine = (j % 2) == par
        cast_to, mm_from = wbs[par], wbs[1 - par]

        @pl.when(mine & do_cast & do_mm)
        def _():
            tile(xp, rp, op, mm_from)
            cast_to[i] = w[...].astype(BF16)

        @pl.when(mine & do_cast & jnp.logical_not(do_mm))
        def _():
            cast_to[i] = w[...].astype(BF16)

        @pl.when(mine & jnp.logical_not(do_cast) & do_mm)
        def _():
            tile(xp, rp, op, mm_from)

        if has_s:
            @pl.when(mine & (j >= 1) & (i == n_p))
            def _():
                tile(xs, rs, os_, mm_from)


def _mm_panel2(xp, xs, w, layer, *, n=None, tm, spp, res=None, swiglu=False, out_dtype=F32, w_t=False,
               name="matmul"):
    mp, k = xp.shape
    n_all = w.shape[1] if w_t else w.shape[2]
    width = n_all // 2 if swiglu else n_all
    n = width if n is None else n
    n_sub_total = n // SUB
    n_panels = -(-n_sub_total // spp)
    nsub = 2 * spp if swiglu else spp
    n_p = mp // tm
    pw = spp * SUB
    has_s = xs is not None
    has_res = res is not None
    up0 = width // SUB
    assert nsub <= n_p

    def w_idx(j, i):
        ii = jnp.where(j < n_panels, jnp.minimum(i, nsub - 1), nsub - 1)
        jj = jnp.minimum(j, n_panels - 1)
        first = jnp.minimum(jj * spp + jnp.minimum(ii, spp - 1), n_sub_total - 1)
        if not swiglu:
            return (layer, 0, first)
        second = up0 + jnp.minimum(jj * spp + jnp.maximum(ii - spp, 0), n_sub_total - 1)
        return (layer, 0, jnp.where(ii < spp, first, second))

    trow = lambda j, i: jnp.where(j == 0, 0, jnp.minimum(i, n_p - 1))
    pcol = lambda j: jnp.maximum(j - 1, 0)
    in_specs = [pl.BlockSpec((tm, k), lambda j, i: (trow(j, i), 0))]
    args = [xp]
    if has_s:
        ms = xs.shape[0]
        in_specs.append(pl.BlockSpec((ms, k), lambda j, i: (0, 0)))
        args.append(xs)
    if w_t:
        in_specs.append(pl.BlockSpec((pl.Squeezed(), SUB, k), lambda j, i: (w_idx(j, i)[0], w_idx(j, i)[2], 0)))
    else:
        in_specs.append(pl.BlockSpec((pl.Squeezed(), k, SUB), w_idx))
    args.append(w)
    if has_res:
        in_specs.append(pl.BlockSpec((tm, pw), lambda j, i: (trow(j, i), pcol(j))))
        args.append(res[0])
        if has_s:
            in_specs.append(pl.BlockSpec((ms, pw), lambda j, i: (0, pcol(j))))
            args.append(res[1])
    out_specs = [pl.BlockSpec((tm, pw), lambda j, i: (trow(j, i), pcol(j)))]
    out_shape = [jax.ShapeDtypeStruct((mp, n), out_dtype)]
    if has_s:
        out_specs.append(pl.BlockSpec((ms, pw), lambda j, i: (0, pcol(j))))
        out_shape.append(jax.ShapeDtypeStruct((ms, n), out_dtype))
    out = pl.pallas_call(
        functools.partial(_mm_panel2_kernel, has_s=has_s, has_res=has_res, n_p=n_p, nsub=nsub, n_panels=n_panels,
                          swiglu=swiglu, w_t=w_t),
        grid=(n_panels + 1, n_p + (1 if has_s else 0)),
        in_specs=in_specs,
        out_specs=out_specs,
        out_shape=out_shape,
        scratch_shapes=[pltpu.VMEM((nsub, SUB, k) if w_t else (nsub, k, SUB), BF16)] * 2,
        compiler_params=_cp("arbitrary", "arbitrary"),
        name=name,
    )(*args)
    return (out[0], out[1]) if has_s else (out[0], None)


def _ffn(yp, ys, g, w_in, w_out, layer):
    act_p, act_s = _mm_panel2(_rmsnorm(yp, g), _rmsnorm(ys, g), w_in, layer, tm=512, spp=4, swiglu=True,
                              out_dtype=BF16, name="ffn_in")
    return _mm_panel(act_p, act_s, w_out, layer, tm=512, spp=2, res=(yp, ys), w_buffers=1, name="ffn_out")


def _memkv_kernel(x_ref, g_ref, o_ref, *, heads, hd):
    w = heads * hd
    for h in range(heads):
        xh = x_ref[:, h * hd:(h + 1) * hd]
        o_ref[:, h * hd:(h + 1) * hd] = xh * lax.rsqrt(jnp.mean(xh * xh, axis=-1, keepdims=True) + RMS_EPS) * g_ref[...]
    o_ref[:, w:] = x_ref[:, w:]


def _mem_kv(mem2d, g_norm, w_kv, layer, g_k, *, hd):
    m = mem2d.shape[0]
    kv, _ = _mm_panel(_rmsnorm(mem2d, g_norm), None, w_kv, layer, tm=512, spp=8, name="mem_kv")
    n = kv.shape[1]
    tm = 256
    return pl.pallas_call(
        functools.partial(_memkv_kernel, heads=MEM_HEADS, hd=hd),
        grid=(m // tm,),
        in_specs=[pl.BlockSpec((tm, n), lambda i: (i, 0)), pl.BlockSpec((1, hd), lambda i: (0, 0))],
        out_specs=pl.BlockSpec((tm, n), lambda i: (i, 0)),
        out_shape=jax.ShapeDtypeStruct((m, n), F32),
        compiler_params=_cp("parallel"),
        name="mem_kv_norm",
    )(kv, g_k.reshape(1, hd))


def _mem_attend_kernel(*refs, heads, hd):
    q_ref, kv_ref, g_ref = refs[:3]
    o_ref = refs[-1]
    w = heads * hd
    for h in range(heads):
        q = q_ref[:, h * hd:(h + 1) * hd]
        qn = q * lax.rsqrt(jnp.mean(q * q, axis=-1, keepdims=True) + RMS_EPS) * g_ref[...]
        k = kv_ref[0, :, h * hd:(h + 1) * hd]
        v = kv_ref[0, :, w + h * hd:w + (h + 1) * hd]
        s = _dot(qn, k, NT) * (hd ** -0.5)
        p = jnp.exp(s - jnp.max(s, axis=-1, keepdims=True))
        o = _dot(p, v) / jnp.sum(p, axis=-1, keepdims=True)
        o_ref[:, h * hd:(h + 1) * hd] = o.astype(o_ref.dtype)


def _mem_attend(u, q_col, kv, g_q, *, bsz, t, tq, into=None, out_col=0):
    n_mem, w2 = kv.shape[1:]
    w = w2 // 2
    hd = w // MEM_HEADS
    tpb = t // tq
    in_specs = [pl.BlockSpec((tq, w), lambda i: (i, q_col)),
                pl.BlockSpec((1, n_mem, w2), lambda i: (i // tpb, 0, 0)),
                pl.BlockSpec((1, hd), lambda i: (0, 0))]
    args = [u, kv, g_q.reshape(1, hd)]
    if into is None:
        out_shape = jax.ShapeDtypeStruct((bsz * t, w), F32)
        aliases = {}
    else:
        in_specs.append(pl.BlockSpec(memory_space=pl.ANY))
        args.append(into)
        out_shape = jax.ShapeDtypeStruct(into.shape, into.dtype)
        aliases = {3: 0}
    return pl.pallas_call(
        functools.partial(_mem_attend_kernel, heads=MEM_HEADS, hd=hd),
        grid=(bsz * t // tq,),
        in_specs=in_specs,
        out_specs=pl.BlockSpec((tq, w), lambda i: (i, out_col)),
        out_shape=out_shape,
        input_output_aliases=aliases,
        compiler_params=_cp("parallel"),
        name="mem_attend",
    )(*args)


def _cumsum(x, tril_incl_bf16):
    hi = x.astype(BF16)
    lo = (x - hi.astype(F32)).astype(BF16)
    return (lax.dot_general(tril_incl_bf16, hi, NN, preferred_element_type=F32)
            + lax.dot_general(tril_incl_bf16, lo, NN, preferred_element_type=F32))


def _rwkv_kernel(ur, uk, uv, ul, zr, zk, zv, zl, mur, muk, muv, mul, w0, a0, kkp, kap, rkp, lg, lb, w2, a2, g2, s0_ref,
                 o_ref, st_ref, s_scr, pr_scr, pk_scr, pv_scr, pl_scr, *, hp, t_valid):
    c = pl.program_id(2)
    n_chunks = pl.num_programs(2)
    ln = ur.shape[0]
    l2 = 2 * ln
    hd = A_HD
    pairs = range(hp)

    @pl.when(c == 0)
    def _():
        s_scr[...] = s0_ref[0]
        pr_scr[...] = zr[0]
        pk_scr[...] = zk[0]
        pv_scr[...] = zv[0]
        pl_scr[...] = zl[0]

    def shift(x_ref, p_scr, mu_ref):
        x = x_ref[...]
        xp = pltpu.roll(x, 1, axis=0)
        row = lax.broadcasted_iota(jnp.int32, x.shape, 0)
        xp = jnp.where(row == 0, p_scr[...], xp)
        p_scr[...] = x[ln - 1:ln, :]
        return x + (xp - x) * mu_ref[...]

    r = shift(ur, pr_scr, mur)
    k = shift(uk, pk_scr, muk)
    v = shift(uv, pv_scr, muv)
    lo = shift(ul, pl_scr, mul)
    wl = lo[:, 0:LANES]
    al = lo[:, LANES:2 * LANES]
    gl = lo[:, 2 * LANES:2 * LANES + GATE_LORA]

    z = -(w0[...] + _dot(jnp.tanh(wl), w2[...]))
    lw = -jnp.exp(-(jnp.maximum(z, 0.0) + jnp.log(1.0 + jnp.exp(-jnp.abs(z)))) - 0.5)
    asig = _sigmoid(a0[...] + _dot(al, a2[...]))
    gate = _dot(_sigmoid(gl), g2[...])
    kk = k * kkp[...]
    k = k * (1.0 + (asig - 1.0) * kap[...])
    if t_valid < ln:
        valid = lax.broadcasted_iota(jnp.int32, (ln, 1), 0) < t_valid
        lw = jnp.where(valid, lw, 0.0)
        kk = jnp.where(valid, kk, 0.0)
        k = jnp.where(valid, k, 0.0)
        v = jnp.where(valid, v, 0.0)

    lane = lax.broadcasted_iota(jnp.int32, (1, LANES), 1)
    m0 = (lane < hd).astype(F32)
    m1 = 1.0 - m0
    row = lax.broadcasted_iota(jnp.int32, (ln, ln), 0)
    col = lax.broadcasted_iota(jnp.int32, (ln, ln), 1)
    tril_incl = (col <= row).astype(BF16)
    row2 = lax.broadcasted_iota(jnp.int32, (l2, l2), 0)
    col2 = lax.broadcasted_iota(jnp.int32, (l2, l2), 1)
    same = (row2 // ln) == (col2 // ln)
    strict2 = same & (col2 < row2)
    incl2 = same & (col2 <= row2)
    eye2 = (row2 == col2).astype(F32)
    rowk = lax.broadcasted_iota(jnp.int32, (LANES, LANES), 0)
    colk = lax.broadcasted_iota(jnp.int32, (LANES, LANES), 1)
    eyek = (rowk == colk).astype(F32)

    def sl(p):
        return slice(p * LANES, (p + 1) * LANES)

    def hsum(x):
        return (jnp.sum(x * m0, axis=-1, keepdims=True) * m0 + jnp.sum(x * m1, axis=-1, keepdims=True) * m1)

    def stack(x):
        return jnp.concatenate([x * m0, x * m1], axis=0)

    def unstack(xs):
        return xs[0:ln] + xs[ln:l2]

    r_ = [r[:, sl(p)] for p in pairs]
    k_ = [k[:, sl(p)] for p in pairs]
    v_ = [v[:, sl(p)] for p in pairs]
    lw_ = [lw[:, sl(p)] for p in pairs]
    kk_ = [kk[:, sl(p)] for p in pairs]
    kkn = [x / jnp.maximum(jnp.sqrt(hsum(x * x)), 1e-12) for x in kk_]
    a_ = [-x for x in kkn]
    b_ = [kkn[p] * asig[:, sl(p)] for p in pairs]

    cin = [_cumsum(x, tril_incl) for x in lw_]
    cl = [x[ln - 1:ln, :] for x in cin]
    inv = [jnp.exp(-x) for x in cin]
    dec = [jnp.exp(cl[p] - cin[p]) for p in pairs]
    a_s = [stack(a_[p] * jnp.exp(cin[p] - lw_[p])) for p in pairs]
    r_s = [stack(r_[p] * jnp.exp(cin[p])) for p in pairs]
    b_s = [stack(b_[p] * inv[p]) for p in pairs]
    k_s = [stack(k_[p] * inv[p]) for p in pairs]
    v_s = [stack(x) for x in v_]
    bh_s = [stack(b_[p] * dec[p]) for p in pairs]
    kh_s = [stack(k_[p] * dec[p]) for p in pairs]

    g = [_dot(jnp.concatenate([a_s[p], r_s[p]], axis=0), jnp.concatenate([b_s[p], k_s[p]], axis=0), NT) for p in pairs]
    a_ab = [jnp.where(strict2, x[0:l2, 0:l2], 0.0) for x in g]
    a_ak = [jnp.where(strict2, x[0:l2, l2:2 * l2], 0.0) for x in g]
    r_b = [jnp.where(incl2, x[l2:2 * l2, 0:l2], 0.0) for x in g]
    r_k = [jnp.where(incl2, x[l2:2 * l2, l2:2 * l2], 0.0) for x in g]

    pw = a_ab
    tinv = [eye2 + x for x in a_ab]
    for _ in range(ln.bit_length() - 2):
        pw = [_dot(x, x) for x in pw]
        tinv = [tinv[p] + _dot(tinv[p], pw[p]) for p in pairs]

    av_s = [_dot(a_ak[p], v_s[p]) for p in pairs]
    wz = [_dot(tinv[p], jnp.concatenate([a_s[p], av_s[p]], axis=1)) for p in pairs]
    w_s = [x[:, 0:LANES] for x in wz]
    z_s = [x[:, LANES:2 * LANES] for x in wz]
    zv_s = [jnp.concatenate([z_s[p], v_s[p]], axis=0) for p in pairs]
    q = [unstack(r_s[p] + _dot(r_b[p], w_s[p])) for p in pairs]
    y_loc = [unstack(_dot(jnp.concatenate([r_b[p], r_k[p]], axis=1), zv_s[p])) for p in pairs]
    mc = [eyek * jnp.exp(cl[p]) + _dot(w_s[p], bh_s[p], TN) for p in pairs]
    nc = [_dot(zv_s[p], jnp.concatenate([bh_s[p], kh_s[p]], axis=0), TN) for p in pairs]

    s = [s_scr[p] for p in pairs]
    y = [_dot(q[p], s[p], NT) + y_loc[p] for p in pairs]
    s_new = [_dot(s[p], mc[p]) + nc[p] for p in pairs]
    for p in pairs:
        s_scr[p] = s_new[p]

    inv_hd = 1.0 / hd
    mean = [hsum(x) * inv_hd for x in y]
    d = [y[p] - mean[p] for p in pairs]
    var = [hsum(x * x) * inv_hd for x in d]
    bonus = [hsum(r_[p] * k_[p] * rkp[:, sl(p)]) * v_[p] for p in pairs]
    for p in pairs:
        yn = d[p] * lax.rsqrt(var[p] + LNX_EPS) * lg[:, sl(p)] + lb[:, sl(p)]
        o_ref[:, sl(p)] = ((yn + bonus[p]) * gate[:, sl(p)]).astype(o_ref.dtype)

    @pl.when(c == n_chunks - 1)
    def _():
        st_ref[0] = s_scr[...]


def _lora_slot(x, axis=-1):
    axis = axis % x.ndim
    parts = jnp.split(x, [DECAY_LORA, DECAY_LORA + AAA_LORA], axis=axis)
    sizes = (LANES, LANES, LORA_PAD - 2 * LANES)
    pad = lambda p, n: jnp.pad(p, [(0, n - p.shape[axis]) if ax == axis else (0, 0) for ax in range(x.ndim)])
    return jnp.concatenate([pad(p, n) for p, n in zip(parts, sizes)], axis=axis)


def _lora_unslot(x):
    return jnp.concatenate([x[..., :DECAY_LORA], x[..., LANES:LANES + AAA_LORA],
                            x[..., 2 * LANES:2 * LANES + GATE_LORA]], axis=-1)


def _rwkv_mixer(u_main, u_tail, prev_main, prev_lora, s0, prm, *, bsz, t, t_valid, hp, out_cols):
    mu_main, mu_lora, w0, a0, kk, ka, rk, lnx_g, lnx_b, w2p, a2p, g2 = prm
    c = w0.shape[0]
    heads = c // A_HD
    n_pairs = heads // 2
    gw = hp * LANES
    ng = c // gw
    nch = t // CHUNK
    s0p = s0.reshape(bsz, n_pairs, 2, A_HD, A_HD)
    zero = jnp.zeros_like(s0p[:, :, 0])
    s0_bd = jnp.concatenate([jnp.concatenate([s0p[:, :, 0], zero], axis=-1),
                             jnp.concatenate([zero, s0p[:, :, 1]], axis=-1)], axis=-2)
    row1 = lambda a: a.reshape(1, -1)
    ublk = lambda off: pl.BlockSpec((CHUNK, gw), lambda bi, gi, ci: (bi * nch + ci, off * ng + gi))
    zblk = lambda off: pl.BlockSpec((1, 1, gw), lambda bi, gi, ci: (bi, 0, off * ng + gi))
    pblk = lambda off: pl.BlockSpec((1, gw), lambda bi, gi, ci: (0, off * ng + gi))
    lora_c = pl.BlockSpec((1, LORA_PAD), lambda bi, gi, ci: (0, 0))
    st_blk = pl.BlockSpec((1, hp, LANES, LANES), lambda bi, gi, ci: (bi, gi, 0, 0))
    in_specs = [
        ublk(0), ublk(1), ublk(2),
        pl.BlockSpec((CHUNK, LORA_PAD), lambda bi, gi, ci: (bi * nch + ci, 0)),
        zblk(0), zblk(1), zblk(2),
        pl.BlockSpec((1, 1, LORA_PAD), lambda bi, gi, ci: (bi, 0, 0)),
        pblk(0), pblk(1), pblk(2), lora_c,
        pblk(0), pblk(0), pblk(0), pblk(0), pblk(0), pblk(0), pblk(0),
        pl.BlockSpec((LANES, gw), lambda bi, gi, ci: (0, gi)),
        pl.BlockSpec((LANES, gw), lambda bi, gi, ci: (0, gi)),
        pl.BlockSpec((GATE_LORA, gw), lambda bi, gi, ci: (0, gi)),
        st_blk,
    ]
    out, st_bd = pl.pallas_call(
        functools.partial(_rwkv_kernel, hp=hp, t_valid=t_valid),
        grid=(bsz, ng, nch),
        in_specs=in_specs,
        out_specs=[pl.BlockSpec((CHUNK, gw), lambda bi, gi, ci: (bi * nch + ci, gi)), st_blk],
        out_shape=[jax.ShapeDtypeStruct((bsz * t, out_cols), BF16),
                   jax.ShapeDtypeStruct((bsz, n_pairs, LANES, LANES), F32)],
        scratch_shapes=[pltpu.VMEM((hp, LANES, LANES), F32), pltpu.VMEM((1, gw), F32), pltpu.VMEM((1, gw), F32),
                        pltpu.VMEM((1, gw), F32), pltpu.VMEM((1, LORA_PAD), F32)],
        compiler_params=_cp("parallel", "parallel", "arbitrary"),
        name="rwkv_mixer",
    )(u_main, u_main, u_main, u_tail, prev_main, prev_main, prev_main, prev_lora,
      row1(mu_main), row1(mu_main), row1(mu_main), row1(mu_lora),
      row1(w0), row1(a0), row1(kk), row1(ka), row1(rk), row1(lnx_g), row1(lnx_b), w2p, a2p, g2, s0_bd)
    st = jnp.stack([st_bd[:, :, :A_HD, :A_HD], st_bd[:, :, A_HD:, A_HD:]], axis=2)
    return out, st.reshape(bsz, heads, A_HD, A_HD)


def _qk_norm(x, g):
    return x * lax.rsqrt(jnp.mean(x * x, axis=-1, keepdims=True) + RMS_EPS) * g


DSWA_NB = 4


def _dswa_prompt_kernel(q0, k0, v0, q1, k1, v1, q2, k2, v2, gq_ref, gk_ref,
                        o_ref, c0, c1, c2, qn_scr, kn_scr, og_scr, lse_scr):
    hd = q0.shape[1]
    t = q0.shape[0]
    scale = hd ** -0.5
    neg = -1e30
    groups = ((q0, k0, v0, c0), (q1, k1, v1, c1), (q2, k2, v2, c2))
    for g, ((q_ref, k_ref, v_ref, c_ref), (win, dil)) in enumerate(zip(groups, B_GROUPS)):
        span = win // dil
        nb = (t // dil) // span
        keep = c_ref.shape[1] // 2
        qn_scr[...] = _qk_norm(q_ref[...], gq_ref[...])
        kn_scr[...] = _qk_norm(k_ref[...], gk_ref[...])
        c_ref[0, pl.ds(0, keep, stride=2), :] = kn_scr[t - keep:t, :]
        c_ref[0, pl.ds(1, keep, stride=2), :] = v_ref[t - keep:t, :]
        row = lax.broadcasted_iota(jnp.int32, (span, span), 0)
        col = lax.broadcasted_iota(jnp.int32, (span, span), 1)
        own_ok = col <= row
        prev_ok = col >= row
        per_it = min(DSWA_NB, nb)
        n_res = DSWA_NB // per_it
        its_per_res = nb // per_it

        def rows_at(start, dil=dil, span=span):
            return pl.ds(start, span) if dil == 1 else pl.ds(start, span, stride=dil)

        def body(it, carry, dil=dil, span=span, g=g, v_ref=v_ref, per_it=per_it, n_res=n_res,
                 its_per_res=its_per_res, own_ok=own_ok, prev_ok=prev_ok, rows_at=rows_at):
            blocks = []
            for rr in range(n_res):
                res = (it // its_per_res) * n_res + rr
                blk0 = (it % its_per_res) * per_it
                for jb in range(per_it):
                    start = res + dil * span * (blk0 + jb)
                    if jb > 0:
                        blocks.append((start, "shared", None))
                    elif its_per_res > 1:
                        blocks.append((start, jnp.maximum(start - dil * span, res), blk0 > 0))
                    else:
                        blocks.append((start, None, None))
            n = len(blocks)
            cur = [rows_at(bk[0]) for bk in blocks]
            qb = [qn_scr[c, :].astype(BF16) for c in cur]
            kc = [kn_scr[c, :].astype(BF16) for c in cur]
            vc = [v_ref[c, :].astype(BF16) for c in cur]
            kp, vp, pmask = [], [], []
            for j, (start, prv, has_prev) in enumerate(blocks):
                if prv is None:
                    kp.append(None)
                    vp.append(None)
                    pmask.append(None)
                elif isinstance(prv, str):
                    kp.append(kc[j - 1])
                    vp.append(vc[j - 1])
                    pmask.append(prev_ok)
                else:
                    rws = rows_at(prv)
                    kp.append(kn_scr[rws, :].astype(BF16))
                    vp.append(v_ref[rws, :].astype(BF16))
                    pmask.append(prev_ok & has_prev)
            s_c = [jnp.where(own_ok, _dot(qb[j], kc[j], NT) * scale, neg) for j in range(n)]
            s_p = [None if kp[j] is None else jnp.where(pmask[j], _dot(qb[j], kp[j], NT) * scale, neg)
                   for j in range(n)]
            m = [jnp.max(s_c[j], axis=-1, keepdims=True) if s_p[j] is None else
                 jnp.maximum(jnp.max(s_c[j], axis=-1, keepdims=True), jnp.max(s_p[j], axis=-1, keepdims=True))
                 for j in range(n)]
            p_c = [jnp.exp(s_c[j] - m[j]) for j in range(n)]
            p_p = [None if s_p[j] is None else jnp.exp(s_p[j] - m[j]) for j in range(n)]
            l = [jnp.sum(p_c[j], axis=-1, keepdims=True) if p_p[j] is None else
                 jnp.sum(p_c[j], axis=-1, keepdims=True) + jnp.sum(p_p[j], axis=-1, keepdims=True)
                 for j in range(n)]
            o = [_dot(p_c[j], vc[j]) if p_p[j] is None else _dot(p_c[j], vc[j]) + _dot(p_p[j], vp[j])
                 for j in range(n)]
            for j in range(n):
                og_scr[g, cur[j], :] = o[j] / l[j]
                lse_scr[g, cur[j], :] = jnp.broadcast_to(m[j] + jnp.log(l[j]), (span, hd))
            return carry

        lax.fori_loop(0, (dil * nb) // DSWA_NB, body, 0)

    l0, l1, l2 = lse_scr[0], lse_scr[1], lse_scr[2]
    m = jnp.maximum(jnp.maximum(l0, l1), l2)
    e0, e1, e2 = jnp.exp(l0 - m), jnp.exp(l1 - m), jnp.exp(l2 - m)
    o = (e0 * og_scr[0] + e1 * og_scr[1] + e2 * og_scr[2]) / (e0 + e1 + e2)
    o_ref[...] = o.astype(o_ref.dtype)


def _dswa_prompt(u, g_q, g_k, *, bsz, t, gh, out_cols):
    hd = B_HD
    ng = len(B_GROUPS)
    nh = ng * gh
    in_specs = []
    for g in range(ng):
        for part in range(3):
            in_specs.append(pl.BlockSpec((t, hd), lambda bi, h, g=g, part=part: (bi, part * nh + g * gh + h)))
    in_specs += [pl.BlockSpec((1, hd), lambda bi, h: (0, 0))] * 2
    keeps = [min(win, t) for win, _ in B_GROUPS]
    res = pl.pallas_call(
        _dswa_prompt_kernel,
        grid=(bsz, gh),
        in_specs=in_specs,
        out_specs=[pl.BlockSpec((t, hd), lambda bi, h: (bi, h))]
        + [pl.BlockSpec((1, 2 * kp, hd), lambda bi, h: (bi, 0, h)) for kp in keeps],
        out_shape=[jax.ShapeDtypeStruct((bsz * t, out_cols), BF16)]
        + [jax.ShapeDtypeStruct((bsz, 2 * kp, gh * hd), F32) for kp in keeps],
        scratch_shapes=[pltpu.VMEM((t, hd), F32), pltpu.VMEM((t, hd), F32),
                        pltpu.VMEM((ng, t, hd), F32), pltpu.VMEM((ng, t, hd), F32)],
        compiler_params=_cp("parallel", "parallel"),
        name="dswa_prompt",
    )(*([u] * 9), g_q.reshape(1, hd), g_k.reshape(1, hd))
    return res[0], [b.reshape(bsz, kp, 2, gh, hd) for b, kp in zip(res[1:], keeps)]


def _dswa_sample_kernel(q0, k0, v0, q1, k1, v1, q2, k2, v2, c0, c1, c2, gq_ref, gk_ref, o_ref, n0, n1, n2):
    hd = q0.shape[1]
    ts = q0.shape[0]
    scale = hd ** -0.5
    neg = -1e30
    groups = ((q0, k0, v0, c0, n0), (q1, k1, v1, c1, n1), (q2, k2, v2, c2, n2))
    outs, lses = [], []
    for (q_ref, k_ref, v_ref, c_ref, n_ref), (win, dil) in zip(groups, B_GROUPS):
        rows = c_ref.shape[1] // 2
        qn = _qk_norm(q_ref[...], gq_ref[...])
        kn = _qk_norm(k_ref[...], gk_ref[...])
        v = v_ref[...]
        kc = c_ref[0, pl.ds(0, rows, stride=2), :]
        vc = c_ref[0, pl.ds(1, rows, stride=2), :]
        s_o = _dot(qn, kc, NT) * scale
        s_n = _dot(qn, kn, NT) * scale
        d_o = rows + lax.broadcasted_iota(jnp.int32, (ts, rows), 0) - lax.broadcasted_iota(jnp.int32, (ts, rows), 1)
        d_n = lax.broadcasted_iota(jnp.int32, (ts, ts), 0) - lax.broadcasted_iota(jnp.int32, (ts, ts), 1)
        ok_o = (d_o <= win) & ((d_o % dil) == 0)
        ok_n = (d_n >= 0) & ((d_n % dil) == 0)
        s_o = jnp.where(ok_o, s_o, neg)
        s_n = jnp.where(ok_n, s_n, neg)
        m = jnp.maximum(jnp.max(s_o, axis=-1, keepdims=True), jnp.max(s_n, axis=-1, keepdims=True))
        p_o = jnp.exp(s_o - m)
        p_n = jnp.exp(s_n - m)
        l = jnp.sum(p_o, axis=-1, keepdims=True) + jnp.sum(p_n, axis=-1, keepdims=True)
        outs.append((_dot(p_o, vc) + _dot(p_n, v)) / l)
        lses.append(m + jnp.log(l))
        n_ref[0, 0:2 * (rows - ts), :] = c_ref[0, 2 * ts:2 * rows, :]
        n_ref[0, pl.ds(2 * (rows - ts), ts, stride=2), :] = kn
        n_ref[0, pl.ds(2 * (rows - ts) + 1, ts, stride=2), :] = v
    m = jnp.maximum(jnp.maximum(lses[0], lses[1]), lses[2])
    es = [jnp.exp(x - m) for x in lses]
    o = (es[0] * outs[0] + es[1] * outs[1] + es[2] * outs[2]) / (es[0] + es[1] + es[2])
    o_ref[...] = o.astype(o_ref.dtype)


def _dswa_sample(u, caches, g_q, g_k, *, bsz, ts):
    hd = B_HD
    ng = len(B_GROUPS)
    gh = caches[0].shape[3]
    nh = ng * gh
    in_specs = []
    for g in range(ng):
        for part in range(3):
            in_specs.append(pl.BlockSpec((ts, hd), lambda bi, h, g=g, part=part: (bi, part * nh + g * gh + h)))
    cache_args, c_specs, c_shapes = [], [], []
    for cch in caches:
        rows = cch.shape[1]
        cache_args.append(cch.reshape(bsz, 2 * rows, gh * hd))
        c_specs.append(pl.BlockSpec((1, 2 * rows, hd), lambda bi, h: (bi, 0, h)))
        c_shapes.append(jax.ShapeDtypeStruct((bsz, 2 * rows, gh * hd), F32))
    in_specs += c_specs + [pl.BlockSpec((1, hd), lambda bi, h: (0, 0))] * 2
    res = pl.pallas_call(
        _dswa_sample_kernel,
        grid=(bsz, gh),
        in_specs=in_specs,
        out_specs=[pl.BlockSpec((ts, hd), lambda bi, h: (bi, h))] + c_specs,
        out_shape=[jax.ShapeDtypeStruct((bsz * ts, gh * hd), F32)] + c_shapes,
        compiler_params=_cp("parallel", "parallel"),
        name="dswa_sample",
    )(*([u] * 9), *cache_args, g_q.reshape(1, hd), g_k.reshape(1, hd))
    return res[0], [b.reshape(cch.shape) for b, cch in zip(res[1:], caches)]


def kernel(x_prompt, x_sample, state_wkv, state_shift, cache_swa_kv1, cache_swa_kv2, cache_swa_kv3, cache_mem_kv, mem_prompt, norm_mix, norm_ffn, norm_mem, w_mem_kv, q_norm_mem, k_norm_mem, w_in_a, w_out_a, mu_a, w0_a, w2_a, a0_a, a2_a, g2_a, kk_a, ka_a, rk_a, lnx_g_a, lnx_b_a, w_in_b, w_out_b, q_norm_b, k_norm_b, w_ffn_in, w_ffn_out):
    bp, t, d = x_prompt.shape
    bs, ts, _ = x_sample.shape
    n_mem = mem_prompt.shape[1]
    mem_w = w_mem_kv.shape[2] // 2
    mem_hd = mem_w // MEM_HEADS
    mix_w = w0_a.shape[1]
    depth = norm_mix.shape[0]
    swa_in = (cache_swa_kv1, cache_swa_kv2, cache_swa_kv3)
    mp_rows = bp * t
    ms_rows = bs * ts
    tm = 512

    yp = x_prompt.reshape(mp_rows, d)
    ys = x_sample.reshape(ms_rows, d)
    mem2d = mem_prompt.reshape(bp * n_mem, d)

    wkv_p, shift_p, wkv_s, shift_s, mem_p = [], [], [], [], []
    swa_p = [[] for _ in B_GROUPS]
    swa_s = [[] for _ in B_GROUPS]
    for i in range(depth):
        j = i // 2
        mkv_p = _mem_kv(mem2d, norm_mem[i], w_mem_kv, i, k_norm_mem[i], hd=mem_hd)
        mem_p.append(mkv_p.reshape(bp, n_mem, 2, MEM_HEADS, mem_hd))
        mkv_p = mkv_p.reshape(bp, n_mem, 2 * mem_w)
        mkv_s = cache_mem_kv[i].reshape(bs, n_mem, 2 * mem_w)
        hp_ = _rmsnorm(yp, norm_mix[i])
        hs_ = _rmsnorm(ys, norm_mix[i])
        if i % 2 == 0:
            rkv_w = 3 * mix_w
            a_shift_w = mu_a.shape[1]
            w_nk = jnp.swapaxes(w_in_a, 1, 2)
            w_tail = jnp.concatenate([_lora_slot(w_nk[j, rkv_w:a_shift_w], axis=0), w_nk[j, a_shift_w:]], axis=0)
            um_p, um_s = _mm_panel2(hp_, hs_, w_nk, j, n=rkv_w, tm=tm, spp=6, w_t=True, name="w_in_a")
            ut_p, ut_s = _mm_panel2(hp_, hs_, w_tail[None], 0, tm=tm, spp=4, w_t=True, name="w_in_a_tail")
            pad_rows = lambda w: jnp.pad(w, ((0, LANES - w.shape[0]), (0, 0)))
            prm = (mu_a[j][:rkv_w], _lora_slot(mu_a[j][rkv_w:]), w0_a[j], a0_a[j], kk_a[j], ka_a[j],
                   rk_a[j].reshape(-1), lnx_g_a[j], lnx_b_a[j], pad_rows(w2_a[j]), pad_rows(a2_a[j]), g2_a[j])
            heads = mix_w // A_HD
            op, sp = _rwkv_mixer(um_p, ut_p, jnp.zeros((bp, 1, rkv_w), F32), jnp.zeros((bp, 1, LORA_PAD), F32),
                                 jnp.zeros((bp, heads, A_HD, A_HD), F32), prm, bsz=bp, t=t, t_valid=CHUNK, hp=8,
                                 out_cols=mix_w + mem_w)
            padt = lambda z: jnp.pad(z.reshape(bs, ts, -1), ((0, 0), (0, CHUNK - ts), (0, 0))).reshape(bs * CHUNK, -1)
            ms, ss = _rwkv_mixer(padt(um_s), padt(ut_s[:, :LORA_PAD]), state_shift[j][:, :, :rkv_w],
                                 _lora_slot(state_shift[j][:, :, rkv_w:]), state_wkv[j], prm, bsz=bs, t=CHUNK,
                                 t_valid=ts, hp=8, out_cols=mix_w)
            ms = ms.reshape(bs, CHUNK, mix_w)[:, :ts].reshape(ms_rows, mix_w)
            op = _mem_attend(ut_p, 1, mkv_p, q_norm_mem[i], bsz=bp, t=t, tq=512, into=op, out_col=mix_w // mem_w)
            as_ = _mem_attend(ut_s, 1, mkv_s, q_norm_mem[i], bsz=bs, t=ts, tq=ts)
            wkv_p.append(sp)
            wkv_s.append(ss)
            last = lambda z, b_, t_: z.reshape(b_, t_, -1)[:, -1:]
            shift_p.append(jnp.concatenate([last(um_p, bp, t), _lora_unslot(last(ut_p[:, :LORA_PAD], bp, t))], axis=-1))
            shift_s.append(jnp.concatenate([last(um_s, bs, ts), _lora_unslot(last(ut_s[:, :LORA_PAD], bs, ts))], axis=-1))
            w_out = w_out_a
        else:
            u_p, u_s = _mm_panel2(hp_, hs_, w_in_b, j, tm=tm, spp=5, name="w_in_b")
            gh = cache_swa_kv1.shape[4]
            att_w = gh * B_HD
            op, bufs_p = _dswa_prompt(u_p, q_norm_b[j], k_norm_b[j], bsz=bp, t=t, gh=gh, out_cols=att_w + mem_w)
            ms, bufs_s = _dswa_sample(u_s, [c[j] for c in swa_in], q_norm_b[j], k_norm_b[j], bsz=bs, ts=ts)
            mem_col = (3 * mix_w) // mem_w
            op = _mem_attend(u_p, mem_col, mkv_p, q_norm_mem[i], bsz=bp, t=t, tq=512, into=op, out_col=att_w // mem_w)
            as_ = _mem_attend(u_s, mem_col, mkv_s, q_norm_mem[i], bsz=bs, t=ts, tq=ts)
            for g in range(len(B_GROUPS)):
                swa_p[g].append(bufs_p[g])
                swa_s[g].append(bufs_s[g])
            w_out = w_out_b
        os_ = jnp.concatenate([ms.astype(BF16), as_.astype(BF16)], axis=1)
        yp, ys = _mm_panel(op, os_, w_out, j, tm=tm, spp=8, res=(yp, ys), name="w_out")
        yp, ys = _ffn(yp, ys, norm_ffn[i], w_ffn_in, w_ffn_out, i)
    return (yp.reshape(bp, t, d), ys.reshape(bs, ts, d),
            jnp.stack(wkv_p), jnp.stack(shift_p),
            jnp.stack(swa_p[0]), jnp.stack(swa_p[1]), jnp.stack(swa_p[2]),
            jnp.stack(mem_p),
            jnp.stack(wkv_s), jnp.stack(shift_s),
            jnp.stack(swa_s[0]), jnp.stack(swa_s[1]), jnp.stack(swa_s[2]))
```

```python
import functools

import jax
import jax.numpy as jnp
from jax import lax
from jax.experimental import pallas as pl
from jax.experimental.pallas import tpu as pltpu

F32 = jnp.float32
BF16 = jnp.bfloat16

RMS_EPS = 1e-6
LNX_EPS = 64e-5

LANES = 128
SUBLANES = 8
VMEM_LIMIT_BYTES = 56 * 2**20

MEM_HEADS = 4
A_HD = 64
B_HD = 128
B_GROUPS = ((128, 1), (512, 4), (2048, 16))
DECAY_LORA = 96
AAA_LORA = 96
GATE_LORA = 384
CHUNK = 64
LORA_PAD = 1024

NT = (((1,), (1,)), ((), ()))
TN = (((0,), (0,)), ((), ()))
NN = (((1,), (0,)), ((), ()))


def _cp(*sem):
    return pltpu.CompilerParams(dimension_semantics=sem, vmem_limit_bytes=VMEM_LIMIT_BYTES)


def _dot(a, b, dims=NN):
    return lax.dot_general(a.astype(BF16), b.astype(BF16), dims, preferred_element_type=F32)


def _sigmoid(x):
    return 1.0 / (1.0 + jnp.exp(-x))


def _rmsnorm_kernel(x_ref, g_ref, o_ref):
    x = x_ref[...]
    y = x * lax.rsqrt(jnp.mean(x * x, axis=-1, keepdims=True) + RMS_EPS)
    o_ref[...] = (y * g_ref[...]).astype(o_ref.dtype)


def _rmsnorm(x, g):
    m, d = x.shape
    tm = min(m, 512)
    return pl.pallas_call(
        _rmsnorm_kernel,
        grid=(m // tm,),
        in_specs=[pl.BlockSpec((tm, d), lambda i: (i, 0)), pl.BlockSpec((1, d), lambda i: (0, 0))],
        out_specs=pl.BlockSpec((tm, d), lambda i: (i, 0)),
        out_shape=jax.ShapeDtypeStruct((m, d), BF16),
        compiler_params=_cp("parallel"),
        name="rmsnorm",
    )(x, g.reshape(1, d))


SUB = 256


def _mm_panel_kernel(*refs, n_xp, has_s, has_res, n_p, nsub, swiglu):
    it = iter(refs)
    xp = [next(it) for _ in range(n_xp)]
    xs = next(it) if has_s else None
    w = next(it)
    rp = next(it) if has_res else None
    rs = next(it) if has_res and has_s else None
    op = next(it)
    os_ = next(it) if has_s else None
    wb = next(it)
    i = pl.program_id(1)
    nout = nsub // 2 if swiglu else nsub

    @pl.when(i < nsub)
    def _():
        wb[i] = w[...].astype(BF16)

    def tile(x_refs, r_ref, o_ref):
        xs_ = [x_ref[...] for x_ref in x_refs]

        def mm(s):
            acc, k0 = None, 0
            for x in xs_:
                part = jnp.dot(x, wb[s, k0:k0 + x.shape[1], :], preferred_element_type=F32)
                acc = part if acc is None else acc + part
                k0 += x.shape[1]
            return acc

        for s in range(nout):
            cols = slice(s * SUB, (s + 1) * SUB)
            acc = mm(s)
            if swiglu:
                acc = acc * _sigmoid(acc) * mm(nout + s)
            if r_ref is not None:
                acc = acc + r_ref[:, cols]
            o_ref[:, cols] = acc.astype(o_ref.dtype)

    @pl.when((i >= nsub) & (i < nsub + n_p))
    def _():
        tile(xp, rp, op)

    if has_s:
        @pl.when(i == nsub + n_p)
        def _():
            tile([xs], rs, os_)


def _mm_panel(xp, xs, w, layer, *, n=None, tm, spp, res=None, swiglu=False, out_dtype=F32, w_buffers=2,
              name="matmul"):
    xps = xp if isinstance(xp, (tuple, list)) else (xp,)
    mp = xps[0].shape[0]
    k = sum(x.shape[1] for x in xps)
    width = w.shape[2] // 2 if swiglu else w.shape[2]
    n = width if n is None else n
    n_sub_total = n // SUB
    n_panels = -(-n_sub_total // spp)
    nsub = 2 * spp if swiglu else spp
    n_p = mp // tm
    pw = spp * SUB
    has_s = xs is not None
    has_res = res is not None
    up0 = width // SUB

    def w_idx(j, i):
        ii = jnp.minimum(i, nsub - 1)
        first = jnp.minimum(j * spp + jnp.minimum(ii, spp - 1), n_sub_total - 1)
        if not swiglu:
            return (layer, 0, first)
        second = up0 + jnp.minimum(j * spp + jnp.maximum(ii - spp, 0), n_sub_total - 1)
        return (layer, 0, jnp.where(ii < spp, first, second))

    prow = lambda j, i: (jnp.clip(i - nsub, 0, n_p - 1), 0)
    pout = lambda j, i: (jnp.clip(i - nsub, 0, n_p - 1), j)
    in_specs = [pl.BlockSpec((tm, x.shape[1]), prow) for x in xps]
    args = list(xps)
    if has_s:
        ms = xs.shape[0]
        in_specs.append(pl.BlockSpec((ms, k), lambda j, i: (0, 0)))
        args.append(xs)
    w_mode = {} if w_buffers == 2 else {"pipeline_mode": pl.Buffered(w_buffers)}
    in_specs.append(pl.BlockSpec((pl.Squeezed(), k, SUB), w_idx, **w_mode))
    args.append(w)
    if has_res:
        in_specs.append(pl.BlockSpec((tm, pw), pout))
        args.append(res[0])
        if has_s:
            in_specs.append(pl.BlockSpec((ms, pw), lambda j, i: (0, j)))
            args.append(res[1])
    out_specs = [pl.BlockSpec((tm, pw), pout)]
    out_shape = [jax.ShapeDtypeStruct((mp, n), out_dtype)]
    if has_s:
        out_specs.append(pl.BlockSpec((ms, pw), lambda j, i: (0, j)))
        out_shape.append(jax.ShapeDtypeStruct((ms, n), out_dtype))
    out = pl.pallas_call(
        functools.partial(_mm_panel_kernel, n_xp=len(xps), has_s=has_s, has_res=has_res, n_p=n_p, nsub=nsub,
                          swiglu=swiglu),
        grid=(n_panels, nsub + n_p + (1 if has_s else 0)),
        in_specs=in_specs,
        out_specs=out_specs,
        out_shape=out_shape,
        scratch_shapes=[pltpu.VMEM((nsub, k, SUB), BF16)],
        compiler_params=_cp("arbitrary", "arbitrary"),
        name=name,
    )(*args)
    return (out[0], out[1]) if has_s else (out[0], None)


def _mm_panel2_kernel(*refs, has_s, has_res, n_p, nsub, n_panels, swiglu, w_t):
    wdims = NT if w_t else NN
    it = iter(refs)
    xp = next(it)
    xs = next(it) if has_s else None
    w = next(it)
    rp = next(it) if has_res else None
    rs = next(it) if has_res and has_s else None
    op = next(it)
    os_ = next(it) if has_s else None
    wbs = (next(it), next(it))
    j = pl.program_id(0)
    i = pl.program_id(1)
    nout = nsub // 2 if swiglu else nsub
    do_cast = (j < n_panels) & (i < nsub)
    do_mm = (j >= 1) & (i < n_p)

    def tile(x_ref, r_ref, o_ref, wb):
        x = x_ref[...]
        for s in range(nout):
            cols = slice(s * SUB, (s + 1) * SUB)
            acc = lax.dot_general(x, wb[s], wdims, preferred_element_type=F32)
            if swiglu:
                acc = acc * _sigmoid(acc) * lax.dot_general(x, wb[nout + s], wdims, preferred_element_type=F32)
            if r_ref is not None:
                acc = acc + r_ref[:, cols]
            o_ref[:, cols] = acc.astype(o_ref.dtype)

    for par in (0, 1):
        mine = (j % 2) == par
        cast_to, mm_from = wbs[par], wbs[1 - par]

        @pl.when(mine & do_cast & do_mm)
        def _():
            tile(xp, rp, op, mm_from)
            cast_to[i] = w[...].astype(BF16)

        @pl.when(mine & do_cast & jnp.logical_not(do_mm))
        def _():
            cast_to[i] = w[...].astype(BF16)

        @pl.when(mine & jnp.logical_not(do_cast) & do_mm)
        def _():
            tile(xp, rp, op, mm_from)

        if has_s:
            @pl.when(mine & (j >= 1) & (i == n_p))
            def _():
                tile(xs, rs, os_, mm_from)


def _mm_panel2(xp, xs, w, layer, *, n=None, tm, spp, res=None, swiglu=False, out_dtype=F32, w_t=False,
               name="matmul"):
    mp, k = xp.shape
    n_all = w.shape[1] if w_t else w.shape[2]
    width = n_all // 2 if swiglu else n_all
    n = width if n is None else n
    n_sub_total = n // SUB
    n_panels = -(-n_sub_total // spp)
    nsub = 2 * spp if swiglu else spp
    n_p = mp // tm
    pw = spp * SUB
    has_s = xs is not None
    has_res = res is not None
    up0 = width // SUB
    assert nsub <= n_p

    def w_idx(j, i):
        ii = jnp.where(j < n_panels, jnp.minimum(i, nsub - 1), nsub - 1)
        jj = jnp.minimum(j, n_panels - 1)
        first = jnp.minimum(jj * spp + jnp.minimum(ii, spp - 1), n_sub_total - 1)
        if not swiglu:
            return (layer, 0, first)
        second = up0 + jnp.minimum(jj * spp + jnp.maximum(ii - spp, 0), n_sub_total - 1)
        return (layer, 0, jnp.where(ii < spp, first, second))

    trow = lambda j, i: jnp.where(j == 0, 0, jnp.minimum(i, n_p - 1))
    pcol = lambda j: jnp.maximum(j - 1, 0)
    in_specs = [pl.BlockSpec((tm, k), lambda j, i: (trow(j, i), 0))]
    args = [xp]
    if has_s:
        ms = xs.shape[0]
        in_specs.append(pl.BlockSpec((ms, k), lambda j, i: (0, 0)))
        args.append(xs)
    if w_t:
        in_specs.append(pl.BlockSpec((pl.Squeezed(), SUB, k), lambda j, i: (w_idx(j, i)[0], w_idx(j, i)[2], 0)))
    else:
        in_specs.append(pl.BlockSpec((pl.Squeezed(), k, SUB), w_idx))
    args.append(w)
    if has_res:
        in_specs.append(pl.BlockSpec((tm, pw), lambda j, i: (trow(j, i), pcol(j))))
        args.append(res[0])
        if has_s:
            in_specs.append(pl.BlockSpec((ms, pw), lambda j, i: (0, pcol(j))))
            args.append(res[1])
    out_specs = [pl.BlockSpec((tm, pw), lambda j, i: (trow(j, i), pcol(j)))]
    out_shape = [jax.ShapeDtypeStruct((mp, n), out_dtype)]
    if has_s:
        out_specs.append(pl.BlockSpec((ms, pw), lambda j, i: (0, pcol(j))))
        out_shape.append(jax.ShapeDtypeStruct((ms, n), out_dtype))
    out = pl.pallas_call(
        functools.partial(_mm_panel2_kernel, has_s=has_s, has_res=has_res, n_p=n_p, nsub=nsub, n_panels=n_panels,
                          swiglu=swiglu, w_t=w_t),
        grid=(n_panels + 1, n_p + (1 if has_s else 0)),
        in_specs=in_specs,
        out_specs=out_specs,
        out_shape=out_shape,
        scratch_shapes=[pltpu.VMEM((nsub, SUB, k) if w_t else (nsub, k, SUB), BF16)] * 2,
        compiler_params=_cp("arbitrary", "arbitrary"),
        name=name,
    )(*args)
    return (out[0], out[1]) if has_s else (out[0], None)


def _ffn(yp, ys, g, w_in, w_out, layer):
    act_p, act_s = _mm_panel2(_rmsnorm(yp, g), _rmsnorm(ys, g), w_in, layer, tm=512, spp=4, swiglu=True,
                              out_dtype=BF16, name="ffn_in")
    return _mm_panel(act_p, act_s, w_out, layer, tm=512, spp=2, res=(yp, ys), w_buffers=1, name="ffn_out")


def _memkv_kernel(x_ref, g_ref, o_ref, *, heads, hd):
    w = heads * hd
    for h in range(heads):
        xh = x_ref[:, h * hd:(h + 1) * hd]
        o_ref[:, h * hd:(h + 1) * hd] = xh * lax.rsqrt(jnp.mean(xh * xh, axis=-1, keepdims=True) + RMS_EPS) * g_ref[...]
    o_ref[:, w:] = x_ref[:, w:]


def _mem_kv(mem2d, g_norm, w_kv, layer, g_k, *, hd):
    m = mem2d.shape[0]
    kv, _ = _mm_panel(_rmsnorm(mem2d, g_norm), None, w_kv, layer, tm=512, spp=8, name="mem_kv")
    n = kv.shape[1]
    tm = 256
    return pl.pallas_call(
        functools.partial(_memkv_kernel, heads=MEM_HEADS, hd=hd),
        grid=(m // tm,),
        in_specs=[pl.BlockSpec((tm, n), lambda i: (i, 0)), pl.BlockSpec((1, hd), lambda i: (0, 0))],
        out_specs=pl.BlockSpec((tm, n), lambda i: (i, 0)),
        out_shape=jax.ShapeDtypeStruct((m, n), F32),
        compiler_params=_cp("parallel"),
        name="mem_kv_norm",
    )(kv, g_k.reshape(1, hd))


def _mem_attend_kernel(q_ref, kv_ref, g_ref, o_ref, *, heads, hd):
    w = heads * hd
    for h in range(heads):
        q = q_ref[:, h * hd:(h + 1) * hd]
        qn = q * lax.rsqrt(jnp.mean(q * q, axis=-1, keepdims=True) + RMS_EPS) * g_ref[...]
        k = kv_ref[0, :, h * hd:(h + 1) * hd]
        v = kv_ref[0, :, w + h * hd:w + (h + 1) * hd]
        s = _dot(qn, k, NT) * (hd ** -0.5)
        p = jnp.exp(s - jnp.max(s, axis=-1, keepdims=True))
        o = _dot(p, v) / jnp.sum(p, axis=-1, keepdims=True)
        o_ref[:, h * hd:(h + 1) * hd] = o.astype(o_ref.dtype)


def _mem_attend(u, q_col, kv, g_q, *, bsz, t, tq, out_dtype):
    n_mem, w2 = kv.shape[1:]
    w = w2 // 2
    hd = w // MEM_HEADS
    tpb = t // tq
    return pl.pallas_call(
        functools.partial(_mem_attend_kernel, heads=MEM_HEADS, hd=hd),
        grid=(bsz * t // tq,),
        in_specs=[pl.BlockSpec((tq, w), lambda i: (i, q_col)),
                  pl.BlockSpec((1, n_mem, w2), lambda i: (i // tpb, 0, 0)),
                  pl.BlockSpec((1, hd), lambda i: (0, 0))],
        out_specs=pl.BlockSpec((tq, w), lambda i: (i, 0)),
        out_shape=jax.ShapeDtypeStruct((bsz * t, w), out_dtype),
        compiler_params=_cp("parallel"),
        name="mem_attend",
    )(u, kv, g_q.reshape(1, hd))


def _cumsum(x, tril_incl_bf16):
    hi = x.astype(BF16)
    lo = (x - hi.astype(F32)).astype(BF16)
    return (lax.dot_general(tril_incl_bf16, hi, NN, preferred_element_type=F32)
            + lax.dot_general(tril_incl_bf16, lo, NN, preferred_element_type=F32))


def _rwkv_kernel(ur, uk, uv, ul, zr, zk, zv, zl, mur, muk, muv, mul, w0, a0, kkp, kap, rkp, lg, lb, w2, a2, g2, s0_ref,
                 o_ref, st_ref, s_scr, pr_scr, pk_scr, pv_scr, pl_scr, *, hp, t_valid):
    c = pl.program_id(2)
    n_chunks = pl.num_programs(2)
    ln = ur.shape[0]
    l2 = 2 * ln
    hd = A_HD
    pairs = range(hp)

    @pl.when(c == 0)
    def _():
        s_scr[...] = s0_ref[0]
        pr_scr[...] = zr[0]
        pk_scr[...] = zk[0]
        pv_scr[...] = zv[0]
        pl_scr[...] = zl[0]

    def shift(x_ref, p_scr, mu_ref):
        x = x_ref[...]
        xp = pltpu.roll(x, 1, axis=0)
        row = lax.broadcasted_iota(jnp.int32, x.shape, 0)
        xp = jnp.where(row == 0, p_scr[...], xp)
        p_scr[...] = x[ln - 1:ln, :]
        return x + (xp - x) * mu_ref[...]

    r = shift(ur, pr_scr, mur)
    k = shift(uk, pk_scr, muk)
    v = shift(uv, pv_scr, muv)
    lo = shift(ul, pl_scr, mul)
    wl = lo[:, 0:LANES]
    al = lo[:, LANES:2 * LANES]
    gl = lo[:, 2 * LANES:2 * LANES + GATE_LORA]

    z = -(w0[...] + _dot(jnp.tanh(wl), w2[...]))
    lw = -jnp.exp(-(jnp.maximum(z, 0.0) + jnp.log(1.0 + jnp.exp(-jnp.abs(z)))) - 0.5)
    asig = _sigmoid(a0[...] + _dot(al, a2[...]))
    gate = _dot(_sigmoid(gl), g2[...])
    kk = k * kkp[...]
    k = k * (1.0 + (asig - 1.0) * kap[...])
    if t_valid < ln:
        valid = lax.broadcasted_iota(jnp.int32, (ln, 1), 0) < t_valid
        lw = jnp.where(valid, lw, 0.0)
        kk = jnp.where(valid, kk, 0.0)
        k = jnp.where(valid, k, 0.0)
        v = jnp.where(valid, v, 0.0)

    lane = lax.broadcasted_iota(jnp.int32, (1, LANES), 1)
    m0 = (lane < hd).astype(F32)
    m1 = 1.0 - m0
    row = lax.broadcasted_iota(jnp.int32, (ln, ln), 0)
    col = lax.broadcasted_iota(jnp.int32, (ln, ln), 1)
    tril_incl = (col <= row).astype(BF16)
    row2 = lax.broadcasted_iota(jnp.int32, (l2, l2), 0)
    col2 = lax.broadcasted_iota(jnp.int32, (l2, l2), 1)
    same = (row2 // ln) == (col2 // ln)
    strict2 = same & (col2 < row2)
    incl2 = same & (col2 <= row2)
    eye2 = (row2 == col2).astype(F32)
    rowk = lax.broadcasted_iota(jnp.int32, (LANES, LANES), 0)
    colk = lax.broadcasted_iota(jnp.int32, (LANES, LANES), 1)
    eyek = (rowk == colk).astype(F32)

    def sl(p):
        return slice(p * LANES, (p + 1) * LANES)

    def hsum(x):
        return (jnp.sum(x * m0, axis=-1, keepdims=True) * m0 + jnp.sum(x * m1, axis=-1, keepdims=True) * m1)

    def stack(x):
        return jnp.concatenate([x * m0, x * m1], axis=0)

    def unstack(xs):
        return xs[0:ln] + xs[ln:l2]

    r_ = [r[:, sl(p)] for p in pairs]
    k_ = [k[:, sl(p)] for p in pairs]
    v_ = [v[:, sl(p)] for p in pairs]
    lw_ = [lw[:, sl(p)] for p in pairs]
    kk_ = [kk[:, sl(p)] for p in pairs]
    kkn = [x / jnp.maximum(jnp.sqrt(hsum(x * x)), 1e-12) for x in kk_]
    a_ = [-x for x in kkn]
    b_ = [kkn[p] * asig[:, sl(p)] for p in pairs]

    cin = [_cumsum(x, tril_incl) for x in lw_]
    cl = [x[ln - 1:ln, :] for x in cin]
    inv = [jnp.exp(-x) for x in cin]
    dec = [jnp.exp(cl[p] - cin[p]) for p in pairs]
    a_s = [stack(a_[p] * jnp.exp(cin[p] - lw_[p])) for p in pairs]
    r_s = [stack(r_[p] * jnp.exp(cin[p])) for p in pairs]
    b_s = [stack(b_[p] * inv[p]) for p in pairs]
    k_s = [stack(k_[p] * inv[p]) for p in pairs]
    v_s = [stack(x) for x in v_]
    bh_s = [stack(b_[p] * dec[p]) for p in pairs]
    kh_s = [stack(k_[p] * dec[p]) for p in pairs]

    g = [_dot(jnp.concatenate([a_s[p], r_s[p]], axis=0), jnp.concatenate([b_s[p], k_s[p]], axis=0), NT) for p in pairs]
    a_ab = [jnp.where(strict2, x[0:l2, 0:l2], 0.0) for x in g]
    a_ak = [jnp.where(strict2, x[0:l2, l2:2 * l2], 0.0) for x in g]
    r_b = [jnp.where(incl2, x[l2:2 * l2, 0:l2], 0.0) for x in g]
    r_k = [jnp.where(incl2, x[l2:2 * l2, l2:2 * l2], 0.0) for x in g]

    pw = a_ab
    tinv = [eye2 + x for x in a_ab]
    for _ in range(ln.bit_length() - 2):
        pw = [_dot(x, x) for x in pw]
        tinv = [tinv[p] + _dot(tinv[p], pw[p]) for p in pairs]

    av_s = [_dot(a_ak[p], v_s[p]) for p in pairs]
    wz = [_dot(tinv[p], jnp.concatenate([a_s[p], av_s[p]], axis=1)) for p in pairs]
    w_s = [x[:, 0:LANES] for x in wz]
    z_s = [x[:, LANES:2 * LANES] for x in wz]
    zv_s = [jnp.concatenate([z_s[p], v_s[p]], axis=0) for p in pairs]
    q = [unstack(r_s[p] + _dot(r_b[p], w_s[p])) for p in pairs]
    y_loc = [unstack(_dot(jnp.concatenate([r_b[p], r_k[p]], axis=1), zv_s[p])) for p in pairs]
    mc = [eyek * jnp.exp(cl[p]) + _dot(w_s[p], bh_s[p], TN) for p in pairs]
    nc = [_dot(zv_s[p], jnp.concatenate([bh_s[p], kh_s[p]], axis=0), TN) for p in pairs]

    s = [s_scr[p] for p in pairs]
    y = [_dot(q[p], s[p], NT) + y_loc[p] for p in pairs]
    s_new = [_dot(s[p], mc[p]) + nc[p] for p in pairs]
    for p in pairs:
        s_scr[p] = s_new[p]

    inv_hd = 1.0 / hd
    mean = [hsum(x) * inv_hd for x in y]
    d = [y[p] - mean[p] for p in pairs]
    var = [hsum(x * x) * inv_hd for x in d]
    bonus = [hsum(r_[p] * k_[p] * rkp[:, sl(p)]) * v_[p] for p in pairs]
    for p in pairs:
        yn = d[p] * lax.rsqrt(var[p] + LNX_EPS) * lg[:, sl(p)] + lb[:, sl(p)]
        o_ref[:, sl(p)] = ((yn + bonus[p]) * gate[:, sl(p)]).astype(o_ref.dtype)

    @pl.when(c == n_chunks - 1)
    def _():
        st_ref[0] = s_scr[...]


def _lora_slot(x, axis=-1):
    axis = axis % x.ndim
    parts = jnp.split(x, [DECAY_LORA, DECAY_LORA + AAA_LORA], axis=axis)
    sizes = (LANES, LANES, LORA_PAD - 2 * LANES)
    pad = lambda p, n: jnp.pad(p, [(0, n - p.shape[axis]) if ax == axis else (0, 0) for ax in range(x.ndim)])
    return jnp.concatenate([pad(p, n) for p, n in zip(parts, sizes)], axis=axis)


def _lora_unslot(x):
    return jnp.concatenate([x[..., :DECAY_LORA], x[..., LANES:LANES + AAA_LORA],
                            x[..., 2 * LANES:2 * LANES + GATE_LORA]], axis=-1)


def _rwkv_mixer(u_main, u_tail, prev_main, prev_lora, s0, prm, *, bsz, t, t_valid, hp):
    mu_main, mu_lora, w0, a0, kk, ka, rk, lnx_g, lnx_b, w2p, a2p, g2 = prm
    c = w0.shape[0]
    heads = c // A_HD
    n_pairs = heads // 2
    gw = hp * LANES
    ng = c // gw
    nch = t // CHUNK
    s0p = s0.reshape(bsz, n_pairs, 2, A_HD, A_HD)
    zero = jnp.zeros_like(s0p[:, :, 0])
    s0_bd = jnp.concatenate([jnp.concatenate([s0p[:, :, 0], zero], axis=-1),
                             jnp.concatenate([zero, s0p[:, :, 1]], axis=-1)], axis=-2)
    row1 = lambda a: a.reshape(1, -1)
    ublk = lambda off: pl.BlockSpec((CHUNK, gw), lambda bi, gi, ci: (bi * nch + ci, off * ng + gi))
    zblk = lambda off: pl.BlockSpec((1, 1, gw), lambda bi, gi, ci: (bi, 0, off * ng + gi))
    pblk = lambda off: pl.BlockSpec((1, gw), lambda bi, gi, ci: (0, off * ng + gi))
    lora_c = pl.BlockSpec((1, LORA_PAD), lambda bi, gi, ci: (0, 0))
    st_blk = pl.BlockSpec((1, hp, LANES, LANES), lambda bi, gi, ci: (bi, gi, 0, 0))
    in_specs = [
        ublk(0), ublk(1), ublk(2),
        pl.BlockSpec((CHUNK, LORA_PAD), lambda bi, gi, ci: (bi * nch + ci, 0)),
        zblk(0), zblk(1), zblk(2),
        pl.BlockSpec((1, 1, LORA_PAD), lambda bi, gi, ci: (bi, 0, 0)),
        pblk(0), pblk(1), pblk(2), lora_c,
        pblk(0), pblk(0), pblk(0), pblk(0), pblk(0), pblk(0), pblk(0),
        pl.BlockSpec((LANES, gw), lambda bi, gi, ci: (0, gi)),
        pl.BlockSpec((LANES, gw), lambda bi, gi, ci: (0, gi)),
        pl.BlockSpec((GATE_LORA, gw), lambda bi, gi, ci: (0, gi)),
        st_blk,
    ]
    out, st_bd = pl.pallas_call(
        functools.partial(_rwkv_kernel, hp=hp, t_valid=t_valid),
        grid=(bsz, ng, nch),
        in_specs=in_specs,
        out_specs=[pl.BlockSpec((CHUNK, gw), lambda bi, gi, ci: (bi * nch + ci, gi)), st_blk],
        out_shape=[jax.ShapeDtypeStruct((bsz * t, c), BF16),
                   jax.ShapeDtypeStruct((bsz, n_pairs, LANES, LANES), F32)],
        scratch_shapes=[pltpu.VMEM((hp, LANES, LANES), F32), pltpu.VMEM((1, gw), F32), pltpu.VMEM((1, gw), F32),
                        pltpu.VMEM((1, gw), F32), pltpu.VMEM((1, LORA_PAD), F32)],
        compiler_params=_cp("parallel", "parallel", "arbitrary"),
        name="rwkv_mixer",
    )(u_main, u_main, u_main, u_tail, prev_main, prev_main, prev_main, prev_lora,
      row1(mu_main), row1(mu_main), row1(mu_main), row1(mu_lora),
      row1(w0), row1(a0), row1(kk), row1(ka), row1(rk), row1(lnx_g), row1(lnx_b), w2p, a2p, g2, s0_bd)
    st = jnp.stack([st_bd[:, :, :A_HD, :A_HD], st_bd[:, :, A_HD:, A_HD:]], axis=2)
    return out, st.reshape(bsz, heads, A_HD, A_HD)


def _qk_norm(x, g):
    return x * lax.rsqrt(jnp.mean(x * x, axis=-1, keepdims=True) + RMS_EPS) * g


DSWA_NB = 4


def _dswa_prompt_kernel(q0, k0, v0, q1, k1, v1, q2, k2, v2, gq_ref, gk_ref,
                        o_ref, c0, c1, c2, qn_scr, kn_scr, og_scr, lse_scr):
    hd = q0.shape[1]
    t = q0.shape[0]
    scale = hd ** -0.5
    neg = -1e30
    groups = ((q0, k0, v0, c0), (q1, k1, v1, c1), (q2, k2, v2, c2))
    for g, ((q_ref, k_ref, v_ref, c_ref), (win, dil)) in enumerate(zip(groups, B_GROUPS)):
        span = win // dil
        nb = (t // dil) // span
        keep = c_ref.shape[1] // 2
        qn_scr[...] = _qk_norm(q_ref[...], gq_ref[...])
        kn_scr[...] = _qk_norm(k_ref[...], gk_ref[...])
        c_ref[0, pl.ds(0, keep, stride=2), :] = kn_scr[t - keep:t, :]
        c_ref[0, pl.ds(1, keep, stride=2), :] = v_ref[t - keep:t, :]
        row = lax.broadcasted_iota(jnp.int32, (span, span), 0)
        col = lax.broadcasted_iota(jnp.int32, (span, span), 1)
        own_ok = col <= row
        prev_ok = col >= row
        per_it = min(DSWA_NB, nb)
        n_res = DSWA_NB // per_it
        its_per_res = nb // per_it

        def rows_at(start, dil=dil, span=span):
            return pl.ds(start, span) if dil == 1 else pl.ds(start, span, stride=dil)

        def body(it, carry, dil=dil, span=span, g=g, v_ref=v_ref, per_it=per_it, n_res=n_res,
                 its_per_res=its_per_res, own_ok=own_ok, prev_ok=prev_ok, rows_at=rows_at):
            blocks = []
            for rr in range(n_res):
                res = (it // its_per_res) * n_res + rr
                blk0 = (it % its_per_res) * per_it
                for jb in range(per_it):
                    start = res + dil * span * (blk0 + jb)
                    if jb > 0:
                        blocks.append((start, "shared", None))
                    elif its_per_res > 1:
                        blocks.append((start, jnp.maximum(start - dil * span, res), blk0 > 0))
                    else:
                        blocks.append((start, None, None))
            n = len(blocks)
            cur = [rows_at(bk[0]) for bk in blocks]
            qb = [qn_scr[c, :].astype(BF16) for c in cur]
            kc = [kn_scr[c, :].astype(BF16) for c in cur]
            vc = [v_ref[c, :].astype(BF16) for c in cur]
            kp, vp, pmask = [], [], []
            for j, (start, prv, has_prev) in enumerate(blocks):
                if prv is None:
                    kp.append(None)
                    vp.append(None)
                    pmask.append(None)
                elif isinstance(prv, str):
                    kp.append(kc[j - 1])
                    vp.append(vc[j - 1])
                    pmask.append(prev_ok)
                else:
                    rws = rows_at(prv)
                    kp.append(kn_scr[rws, :].astype(BF16))
                    vp.append(v_ref[rws, :].astype(BF16))
                    pmask.append(prev_ok & has_prev)
            s_c = [jnp.where(own_ok, _dot(qb[j], kc[j], NT) * scale, neg) for j in range(n)]
            s_p = [None if kp[j] is None else jnp.where(pmask[j], _dot(qb[j], kp[j], NT) * scale, neg)
                   for j in range(n)]
            m = [jnp.max(s_c[j], axis=-1, keepdims=True) if s_p[j] is None else
                 jnp.maximum(jnp.max(s_c[j], axis=-1, keepdims=True), jnp.max(s_p[j], axis=-1, keepdims=True))
                 for j in range(n)]
            p_c = [jnp.exp(s_c[j] - m[j]) for j in range(n)]
            p_p = [None if s_p[j] is None else jnp.exp(s_p[j] - m[j]) for j in range(n)]
            l = [jnp.sum(p_c[j], axis=-1, keepdims=True) if p_p[j] is None else
                 jnp.sum(p_c[j], axis=-1, keepdims=True) + jnp.sum(p_p[j], axis=-1, keepdims=True)
                 for j in range(n)]
            o = [_dot(p_c[j], vc[j]) if p_p[j] is None else _dot(p_c[j], vc[j]) + _dot(p_p[j], vp[j])
                 for j in range(n)]
            for j in range(n):
                og_scr[g, cur[j], :] = o[j] / l[j]
                lse_scr[g, cur[j], :] = jnp.broadcast_to(m[j] + jnp.log(l[j]), (span, hd))
            return carry

        lax.fori_loop(0, (dil * nb) // DSWA_NB, body, 0)

    l0, l1, l2 = lse_scr[0], lse_scr[1], lse_scr[2]
    m = jnp.maximum(jnp.maximum(l0, l1), l2)
    e0, e1, e2 = jnp.exp(l0 - m), jnp.exp(l1 - m), jnp.exp(l2 - m)
    o = (e0 * og_scr[0] + e1 * og_scr[1] + e2 * og_scr[2]) / (e0 + e1 + e2)
    o_ref[...] = o.astype(o_ref.dtype)


def _dswa_prompt(u, g_q, g_k, *, bsz, t, gh):
    hd = B_HD
    ng = len(B_GROUPS)
    nh = ng * gh
    in_specs = []
    for g in range(ng):
        for part in range(3):
            in_specs.append(pl.BlockSpec((t, hd), lambda bi, h, g=g, part=part: (bi, part * nh + g * gh + h)))
    in_specs += [pl.BlockSpec((1, hd), lambda bi, h: (0, 0))] * 2
    keeps = [min(win, t) for win, _ in B_GROUPS]
    res = pl.pallas_call(
        _dswa_prompt_kernel,
        grid=(bsz, gh),
        in_specs=in_specs,
        out_specs=[pl.BlockSpec((t, hd), lambda bi, h: (bi, h))]
        + [pl.BlockSpec((1, 2 * kp, hd), lambda bi, h: (bi, 0, h)) for kp in keeps],
        out_shape=[jax.ShapeDtypeStruct((bsz * t, gh * hd), BF16)]
        + [jax.ShapeDtypeStruct((bsz, 2 * kp, gh * hd), F32) for kp in keeps],
        scratch_shapes=[pltpu.VMEM((t, hd), F32), pltpu.VMEM((t, hd), F32),
                        pltpu.VMEM((ng, t, hd), F32), pltpu.VMEM((ng, t, hd), F32)],
        compiler_params=_cp("parallel", "parallel"),
        name="dswa_prompt",
    )(*([u] * 9), g_q.reshape(1, hd), g_k.reshape(1, hd))
    return res[0], [b.reshape(bsz, kp, 2, gh, hd) for b, kp in zip(res[1:], keeps)]


def _dswa_sample_kernel(q0, k0, v0, q1, k1, v1, q2, k2, v2, c0, c1, c2, gq_ref, gk_ref, o_ref, n0, n1, n2):
    hd = q0.shape[1]
    ts = q0.shape[0]
    scale = hd ** -0.5
    neg = -1e30
    groups = ((q0, k0, v0, c0, n0), (q1, k1, v1, c1, n1), (q2, k2, v2, c2, n2))
    outs, lses = [], []
    for (q_ref, k_ref, v_ref, c_ref, n_ref), (win, dil) in zip(groups, B_GROUPS):
        rows = c_ref.shape[1] // 2
        qn = _qk_norm(q_ref[...], gq_ref[...])
        kn = _qk_norm(k_ref[...], gk_ref[...])
        v = v_ref[...]
        kc = c_ref[0, pl.ds(0, rows, stride=2), :]
        vc = c_ref[0, pl.ds(1, rows, stride=2), :]
        s_o = _dot(qn, kc, NT) * scale
        s_n = _dot(qn, kn, NT) * scale
        d_o = rows + lax.broadcasted_iota(jnp.int32, (ts, rows), 0) - lax.broadcasted_iota(jnp.int32, (ts, rows), 1)
        d_n = lax.broadcasted_iota(jnp.int32, (ts, ts), 0) - lax.broadcasted_iota(jnp.int32, (ts, ts), 1)
        ok_o = (d_o <= win) & ((d_o % dil) == 0)
        ok_n = (d_n >= 0) & ((d_n % dil) == 0)
        s_o = jnp.where(ok_o, s_o, neg)
        s_n = jnp.where(ok_n, s_n, neg)
        m = jnp.maximum(jnp.max(s_o, axis=-1, keepdims=True), jnp.max(s_n, axis=-1, keepdims=True))
        p_o = jnp.exp(s_o - m)
        p_n = jnp.exp(s_n - m)
        l = jnp.sum(p_o, axis=-1, keepdims=True) + jnp.sum(p_n, axis=-1, keepdims=True)
        outs.append((_dot(p_o, vc) + _dot(p_n, v)) / l)
        lses.append(m + jnp.log(l))
        n_ref[0, 0:2 * (rows - ts), :] = c_ref[0, 2 * ts:2 * rows, :]
        n_ref[0, pl.ds(2 * (rows - ts), ts, stride=2), :] = kn
        n_ref[0, pl.ds(2 * (rows - ts) + 1, ts, stride=2), :] = v
    m = jnp.maximum(jnp.maximum(lses[0], lses[1]), lses[2])
    es = [jnp.exp(x - m) for x in lses]
    o = (es[0] * outs[0] + es[1] * outs[1] + es[2] * outs[2]) / (es[0] + es[1] + es[2])
    o_ref[...] = o.astype(o_ref.dtype)


def _dswa_sample(u, caches, g_q, g_k, *, bsz, ts):
    hd = B_HD
    ng = len(B_GROUPS)
    gh = caches[0].shape[3]
    nh = ng * gh
    in_specs = []
    for g in range(ng):
        for part in range(3):
            in_specs.append(pl.BlockSpec((ts, hd), lambda bi, h, g=g, part=part: (bi, part * nh + g * gh + h)))
    cache_args, c_specs, c_shapes = [], [], []
    for cch in caches:
        rows = cch.shape[1]
        cache_args.append(cch.reshape(bsz, 2 * rows, gh * hd))
        c_specs.append(pl.BlockSpec((1, 2 * rows, hd), lambda bi, h: (bi, 0, h)))
        c_shapes.append(jax.ShapeDtypeStruct((bsz, 2 * rows, gh * hd), F32))
    in_specs += c_specs + [pl.BlockSpec((1, hd), lambda bi, h: (0, 0))] * 2
    res = pl.pallas_call(
        _dswa_sample_kernel,
        grid=(bsz, gh),
        in_specs=in_specs,
        out_specs=[pl.BlockSpec((ts, hd), lambda bi, h: (bi, h))] + c_specs,
        out_shape=[jax.ShapeDtypeStruct((bsz * ts, gh * hd), F32)] + c_shapes,
        compiler_params=_cp("parallel", "parallel"),
        name="dswa_sample",
    )(*([u] * 9), *cache_args, g_q.reshape(1, hd), g_k.reshape(1, hd))
    return res[0], [b.reshape(cch.shape) for b, cch in zip(res[1:], caches)]


def kernel(x_prompt, x_sample, state_wkv, state_shift, cache_swa_kv1, cache_swa_kv2, cache_swa_kv3, cache_mem_kv, mem_prompt, norm_mix, norm_ffn, norm_mem, w_mem_kv, q_norm_mem, k_norm_mem, w_in_a, w_out_a, mu_a, w0_a, w2_a, a0_a, a2_a, g2_a, kk_a, ka_a, rk_a, lnx_g_a, lnx_b_a, w_in_b, w_out_b, q_norm_b, k_norm_b, w_ffn_in, w_ffn_out):
    bp, t, d = x_prompt.shape
    bs, ts, _ = x_sample.shape
    n_mem = mem_prompt.shape[1]
    mem_w = w_mem_kv.shape[2] // 2
    mem_hd = mem_w // MEM_HEADS
    mix_w = w0_a.shape[1]
    depth = norm_mix.shape[0]
    swa_in = (cache_swa_kv1, cache_swa_kv2, cache_swa_kv3)
    mp_rows = bp * t
    ms_rows = bs * ts
    tm = 512

    yp = x_prompt.reshape(mp_rows, d)
    ys = x_sample.reshape(ms_rows, d)
    mem2d = mem_prompt.reshape(bp * n_mem, d)

    wkv_p, shift_p, wkv_s, shift_s, mem_p = [], [], [], [], []
    swa_p = [[] for _ in B_GROUPS]
    swa_s = [[] for _ in B_GROUPS]
    for i in range(depth):
        j = i // 2
        mkv_p = _mem_kv(mem2d, norm_mem[i], w_mem_kv, i, k_norm_mem[i], hd=mem_hd)
        mem_p.append(mkv_p.reshape(bp, n_mem, 2, MEM_HEADS, mem_hd))
        mkv_p = mkv_p.reshape(bp, n_mem, 2 * mem_w)
        mkv_s = cache_mem_kv[i].reshape(bs, n_mem, 2 * mem_w)
        hp_ = _rmsnorm(yp, norm_mix[i])
        hs_ = _rmsnorm(ys, norm_mix[i])
        if i % 2 == 0:
            rkv_w = 3 * mix_w
            a_shift_w = mu_a.shape[1]
            w_nk = jnp.swapaxes(w_in_a, 1, 2)
            w_tail = jnp.concatenate([_lora_slot(w_nk[j, rkv_w:a_shift_w], axis=0), w_nk[j, a_shift_w:]], axis=0)
            um_p, um_s = _mm_panel2(hp_, hs_, w_nk, j, n=rkv_w, tm=tm, spp=6, w_t=True, name="w_in_a")
            ut_p, ut_s = _mm_panel2(hp_, hs_, w_tail[None], 0, tm=tm, spp=4, w_t=True, name="w_in_a_tail")
            pad_rows = lambda w: jnp.pad(w, ((0, LANES - w.shape[0]), (0, 0)))
            prm = (mu_a[j][:rkv_w], _lora_slot(mu_a[j][rkv_w:]), w0_a[j], a0_a[j], kk_a[j], ka_a[j],
                   rk_a[j].reshape(-1), lnx_g_a[j], lnx_b_a[j], pad_rows(w2_a[j]), pad_rows(a2_a[j]), g2_a[j])
            heads = mix_w // A_HD
            mp, sp = _rwkv_mixer(um_p, ut_p, jnp.zeros((bp, 1, rkv_w), F32), jnp.zeros((bp, 1, LORA_PAD), F32),
                                 jnp.zeros((bp, heads, A_HD, A_HD), F32), prm, bsz=bp, t=t, t_valid=CHUNK, hp=24)
            padt = lambda z: jnp.pad(z.reshape(bs, ts, -1), ((0, 0), (0, CHUNK - ts), (0, 0))).reshape(bs * CHUNK, -1)
            ms, ss = _rwkv_mixer(padt(um_s), padt(ut_s[:, :LORA_PAD]), state_shift[j][:, :, :rkv_w],
                                 _lora_slot(state_shift[j][:, :, rkv_w:]), state_wkv[j], prm, bsz=bs, t=CHUNK,
                                 t_valid=ts, hp=24)
            ms = ms.reshape(bs, CHUNK, mix_w)[:, :ts].reshape(ms_rows, mix_w)
            ap = _mem_attend(ut_p, 1, mkv_p, q_norm_mem[i], bsz=bp, t=t, tq=512, out_dtype=BF16)
            as_ = _mem_attend(ut_s, 1, mkv_s, q_norm_mem[i], bsz=bs, t=ts, tq=ts, out_dtype=F32)
            wkv_p.append(sp)
            wkv_s.append(ss)
            last = lambda z, b_, t_: z.reshape(b_, t_, -1)[:, -1:]
            shift_p.append(jnp.concatenate([last(um_p, bp, t), _lora_unslot(last(ut_p[:, :LORA_PAD], bp, t))], axis=-1))
            shift_s.append(jnp.concatenate([last(um_s, bs, ts), _lora_unslot(last(ut_s[:, :LORA_PAD], bs, ts))], axis=-1))
            w_out = w_out_a
        else:
            u_p, u_s = _mm_panel2(hp_, hs_, w_in_b, j, tm=tm, spp=5, name="w_in_b")
            gh = cache_swa_kv1.shape[4]
            att_w = gh * B_HD
            mp, bufs_p = _dswa_prompt(u_p, q_norm_b[j], k_norm_b[j], bsz=bp, t=t, gh=gh)
            ms, bufs_s = _dswa_sample(u_s, [c[j] for c in swa_in], q_norm_b[j], k_norm_b[j], bsz=bs, ts=ts)
            mem_col = (3 * mix_w) // mem_w
            ap = _mem_attend(u_p, mem_col, mkv_p, q_norm_mem[i], bsz=bp, t=t, tq=512, out_dtype=BF16)
            as_ = _mem_attend(u_s, mem_col, mkv_s, q_norm_mem[i], bsz=bs, t=ts, tq=ts, out_dtype=F32)
            for g in range(len(B_GROUPS)):
                swa_p[g].append(bufs_p[g])
                swa_s[g].append(bufs_s[g])
            w_out = w_out_b
        os_ = jnp.concatenate([ms.astype(BF16), as_.astype(BF16)], axis=1)
        yp, ys = _mm_panel((mp, ap), os_, w_out, j, tm=tm, spp=8, res=(yp, ys), name="w_out")
        yp, ys = _ffn(yp, ys, norm_ffn[i], w_ffn_in, w_ffn_out, i)
    return (yp.reshape(bp, t, d), ys.reshape(bs, ts, d),
            jnp.stack(wkv_p), jnp.stack(shift_p),
            jnp.stack(swa_p[0]), jnp.stack(swa_p[1]), jnp.stack(swa_p[2]),
            jnp.stack(mem_p),
            jnp.stack(wkv_s), jnp.stack(shift_s),
            jnp.stack(swa_s[0]), jnp.stack(swa_s[1]), jnp.stack(swa_s[2]))
```

```python
import functools

import jax
import jax.numpy as jnp
from jax import lax
from jax.experimental import pallas as pl
from jax.experimental.pallas import tpu as pltpu

F32 = jnp.float32
BF16 = jnp.bfloat16

RMS_EPS = 1e-6
LNX_EPS = 64e-5

LANES = 128
SUBLANES = 8
VMEM_LIMIT_BYTES = 56 * 2**20

MEM_HEADS = 4
A_HD = 64
B_HD = 128
B_GROUPS = ((128, 1), (512, 4), (2048, 16))
DECAY_LORA = 96
AAA_LORA = 96
GATE_LORA = 384
CHUNK = 64
LORA_PAD = 1024

NT = (((1,), (1,)), ((), ()))
TN = (((0,), (0,)), ((), ()))
NN = (((1,), (0,)), ((), ()))


def _cp(*sem):
    return pltpu.CompilerParams(dimension_semantics=sem, vmem_limit_bytes=VMEM_LIMIT_BYTES)


def _dot(a, b, dims=NN):
    return lax.dot_general(a.astype(BF16), b.astype(BF16), dims, preferred_element_type=F32)


def _sigmoid(x):
    return 1.0 / (1.0 + jnp.exp(-x))


def _rmsnorm_kernel(x_ref, g_ref, o_ref):
    x = x_ref[...]
    y = x * lax.rsqrt(jnp.mean(x * x, axis=-1, keepdims=True) + RMS_EPS)
    o_ref[...] = (y * g_ref[...]).astype(o_ref.dtype)


def _rmsnorm(x, g):
    m, d = x.shape
    tm = min(m, 512)
    return pl.pallas_call(
        _rmsnorm_kernel,
        grid=(m // tm,),
        in_specs=[pl.BlockSpec((tm, d), lambda i: (i, 0)), pl.BlockSpec((1, d), lambda i: (0, 0))],
        out_specs=pl.BlockSpec((tm, d), lambda i: (i, 0)),
        out_shape=jax.ShapeDtypeStruct((m, d), BF16),
        compiler_params=_cp("parallel"),
        name="rmsnorm",
    )(x, g.reshape(1, d))


SUB = 256


def _mm_panel_kernel(*refs, n_xp, has_s, has_res, n_p, nsub, swiglu):
    it = iter(refs)
    xp = [next(it) for _ in range(n_xp)]
    xs = next(it) if has_s else None
    w = next(it)
    rp = next(it) if has_res else None
    rs = next(it) if has_res and has_s else None
    op = next(it)
    os_ = next(it) if has_s else None
    wb = next(it)
    i = pl.program_id(1)
    nout = nsub // 2 if swiglu else nsub

    @pl.when(i < nsub)
    def _():
        wb[i] = w[...].astype(BF16)

    def tile(x_refs, r_ref, o_ref):
        xs_ = [x_ref[...] for x_ref in x_refs]

        def mm(s):
            acc, k0 = None, 0
            for x in xs_:
                part = jnp.dot(x, wb[s, k0:k0 + x.shape[1], :], preferred_element_type=F32)
                acc = part if acc is None else acc + part
                k0 += x.shape[1]
            return acc

        for s in range(nout):
            cols = slice(s * SUB, (s + 1) * SUB)
            acc = mm(s)
            if swiglu:
                acc = acc * _sigmoid(acc) * mm(nout + s)
            if r_ref is not None:
                acc = acc + r_ref[:, cols]
            o_ref[:, cols] = acc.astype(o_ref.dtype)

    @pl.when((i >= nsub) & (i < nsub + n_p))
    def _():
        tile(xp, rp, op)

    if has_s:
        @pl.when(i == nsub + n_p)
        def _():
            tile([xs], rs, os_)


def _mm_panel(xp, xs, w, layer, *, n=None, tm, spp, res=None, swiglu=False, out_dtype=F32, w_buffers=2,
              name="matmul"):
    xps = xp if isinstance(xp, (tuple, list)) else (xp,)
    mp = xps[0].shape[0]
    k = sum(x.shape[1] for x in xps)
    width = w.shape[2] // 2 if swiglu else w.shape[2]
    n = width if n is None else n
    n_sub_total = n // SUB
    n_panels = -(-n_sub_total // spp)
    nsub = 2 * spp if swiglu else spp
    n_p = mp // tm
    pw = spp * SUB
    has_s = xs is not None
    has_res = res is not None
    up0 = width // SUB

    def w_idx(j, i):
        ii = jnp.minimum(i, nsub - 1)
        first = jnp.minimum(j * spp + jnp.minimum(ii, spp - 1), n_sub_total - 1)
        if not swiglu:
            return (layer, 0, first)
        second = up0 + jnp.minimum(j * spp + jnp.maximum(ii - spp, 0), n_sub_total - 1)
        return (layer, 0, jnp.where(ii < spp, first, second))

    prow = lambda j, i: (jnp.clip(i - nsub, 0, n_p - 1), 0)
    pout = lambda j, i: (jnp.clip(i - nsub, 0, n_p - 1), j)
    in_specs = [pl.BlockSpec((tm, x.shape[1]), prow) for x in xps]
    args = list(xps)
    if has_s:
        ms = xs.shape[0]
        in_specs.append(pl.BlockSpec((ms, k), lambda j, i: (0, 0)))
        args.append(xs)
    w_mode = {} if w_buffers == 2 else {"pipeline_mode": pl.Buffered(w_buffers)}
    in_specs.append(pl.BlockSpec((pl.Squeezed(), k, SUB), w_idx, **w_mode))
    args.append(w)
    if has_res:
        in_specs.append(pl.BlockSpec((tm, pw), pout))
        args.append(res[0])
        if has_s:
            in_specs.append(pl.BlockSpec((ms, pw), lambda j, i: (0, j)))
            args.append(res[1])
    out_specs = [pl.BlockSpec((tm, pw), pout)]
    out_shape = [jax.ShapeDtypeStruct((mp, n), out_dtype)]
    if has_s:
        out_specs.append(pl.BlockSpec((ms, pw), lambda j, i: (0, j)))
        out_shape.append(jax.ShapeDtypeStruct((ms, n), out_dtype))
    out = pl.pallas_call(
        functools.partial(_mm_panel_kernel, n_xp=len(xps), has_s=has_s, has_res=has_res, n_p=n_p, nsub=nsub,
                          swiglu=swiglu),
        grid=(n_panels, nsub + n_p + (1 if has_s else 0)),
        in_specs=in_specs,
        out_specs=out_specs,
        out_shape=out_shape,
        scratch_shapes=[pltpu.VMEM((nsub, k, SUB), BF16)],
        compiler_params=_cp("arbitrary", "arbitrary"),
        name=name,
    )(*args)
    return (out[0], out[1]) if has_s else (out[0], None)


def _mm_panel2_kernel(*refs, has_s, has_res, n_p, nsub, n_panels, swiglu, w_t):
    wdims = NT if w_t else NN
    it = iter(refs)
    xp = next(it)
    xs = next(it) if has_s else None
    w = next(it)
    rp = next(it) if has_res else None
    rs = next(it) if has_res and has_s else None
    op = next(it)
    os_ = next(it) if has_s else None
    wbs = (next(it), next(it))
    j = pl.program_id(0)
    i = pl.program_id(1)
    nout = nsub // 2 if swiglu else nsub
    do_cast = (j < n_panels) & (i < nsub)
    do_mm = (j >= 1) & (i < n_p)

    def tile(x_ref, r_ref, o_ref, wb):
        x = x_ref[...]
        for s in range(nout):
            cols = slice(s * SUB, (s + 1) * SUB)
            acc = lax.dot_general(x, wb[s], wdims, preferred_element_type=F32)
            if swiglu:
                acc = acc * _sigmoid(acc) * lax.dot_general(x, wb[nout + s], wdims, preferred_element_type=F32)
            if r_ref is not None:
                acc = acc + r_ref[:, cols]
            o_ref[:, cols] = acc.astype(o_ref.dtype)

    for par in (0, 1):
        mine = (j % 2) == par
        cast_to, mm_from = wbs[par], wbs[1 - par]

        @pl.when(mine & do_cast & do_mm)
        def _():
            tile(xp, rp, op, mm_from)
            cast_to[i] = w[...].astype(BF16)

        @pl.when(mine & do_cast & jnp.logical_not(do_mm))
        def _():
            cast_to[i] = w[...].astype(BF16)

        @pl.when(mine & jnp.logical_not(do_cast) & do_mm)
        def _():
            tile(xp, rp, op, mm_from)

        if has_s:
            @pl.when(mine & (j >= 1) & (i == n_p))
            def _():
                tile(xs, rs, os_, mm_from)


def _mm_panel2(xp, xs, w, layer, *, n=None, tm, spp, res=None, swiglu=False, out_dtype=F32, w_t=False,
               name="matmul"):
    mp, k = xp.shape
    n_all = w.shape[1] if w_t else w.shape[2]
    width = n_all // 2 if swiglu else n_all
    n = width if n is None else n
    n_sub_total = n // SUB
    n_panels = -(-n_sub_total // spp)
    nsub = 2 * spp if swiglu else spp
    n_p = mp // tm
    pw = spp * SUB
    has_s = xs is not None
    has_res = res is not None
    up0 = width // SUB
    assert nsub <= n_p

    def w_idx(j, i):
        ii = jnp.where(j < n_panels, jnp.minimum(i, nsub - 1), nsub - 1)
        jj = jnp.minimum(j, n_panels - 1)
        first = jnp.minimum(jj * spp + jnp.minimum(ii, spp - 1), n_sub_total - 1)
        if not swiglu:
            return (layer, 0, first)
        second = up0 + jnp.minimum(jj * spp + jnp.maximum(ii - spp, 0), n_sub_total - 1)
        return (layer, 0, jnp.where(ii < spp, first, second))

    trow = lambda j, i: jnp.where(j == 0, 0, jnp.minimum(i, n_p - 1))
    pcol = lambda j: jnp.maximum(j - 1, 0)
    in_specs = [pl.BlockSpec((tm, k), lambda j, i: (trow(j, i), 0))]
    args = [xp]
    if has_s:
        ms = xs.shape[0]
        in_specs.append(pl.BlockSpec((ms, k), lambda j, i: (0, 0)))
        args.append(xs)
    if w_t:
        in_specs.append(pl.BlockSpec((pl.Squeezed(), SUB, k), lambda j, i: (w_idx(j, i)[0], w_idx(j, i)[2], 0)))
    else:
        in_specs.append(pl.BlockSpec((pl.Squeezed(), k, SUB), w_idx))
    args.append(w)
    if has_res:
        in_specs.append(pl.BlockSpec((tm, pw), lambda j, i: (trow(j, i), pcol(j))))
        args.append(res[0])
        if has_s:
            in_specs.append(pl.BlockSpec((ms, pw), lambda j, i: (0, pcol(j))))
            args.append(res[1])
    out_specs = [pl.BlockSpec((tm, pw), lambda j, i: (trow(j, i), pcol(j)))]
    out_shape = [jax.ShapeDtypeStruct((mp, n), out_dtype)]
    if has_s:
        out_specs.append(pl.BlockSpec((ms, pw), lambda j, i: (0, pcol(j))))
        out_shape.append(jax.ShapeDtypeStruct((ms, n), out_dtype))
    out = pl.pallas_call(
        functools.partial(_mm_panel2_kernel, has_s=has_s, has_res=has_res, n_p=n_p, nsub=nsub, n_panels=n_panels,
                          swiglu=swiglu, w_t=w_t),
        grid=(n_panels + 1, n_p + (1 if has_s else 0)),
        in_specs=in_specs,
        out_specs=out_specs,
        out_shape=out_shape,
        scratch_shapes=[pltpu.VMEM((nsub, SUB, k) if w_t else (nsub, k, SUB), BF16)] * 2,
        compiler_params=_cp("arbitrary", "arbitrary"),
        name=name,
    )(*args)
    return (out[0], out[1]) if has_s else (out[0], None)


def _ffn(yp, ys, g, w_in, w_out, layer):
    act_p, act_s = _mm_panel2(_rmsnorm(yp, g), _rmsnorm(ys, g), w_in, layer, tm=512, spp=4, swiglu=True,
                              out_dtype=BF16, name="ffn_in")
    return _mm_panel(act_p, act_s, w_out, layer, tm=512, spp=2, res=(yp, ys), w_buffers=1, name="ffn_out")


def _memkv_kernel(x_ref, g_ref, o_ref, *, heads, hd):
    w = heads * hd
    for h in range(heads):
        xh = x_ref[:, h * hd:(h + 1) * hd]
        o_ref[:, h * hd:(h + 1) * hd] = xh * lax.rsqrt(jnp.mean(xh * xh, axis=-1, keepdims=True) + RMS_EPS) * g_ref[...]
    o_ref[:, w:] = x_ref[:, w:]


def _mem_kv(mem2d, g_norm, w_kv, layer, g_k, *, hd):
    m = mem2d.shape[0]
    kv, _ = _mm_panel(_rmsnorm(mem2d, g_norm), None, w_kv, layer, tm=512, spp=8, name="mem_kv")
    n = kv.shape[1]
    tm = 256
    return pl.pallas_call(
        functools.partial(_memkv_kernel, heads=MEM_HEADS, hd=hd),
        grid=(m // tm,),
        in_specs=[pl.BlockSpec((tm, n), lambda i: (i, 0)), pl.BlockSpec((1, hd), lambda i: (0, 0))],
        out_specs=pl.BlockSpec((tm, n), lambda i: (i, 0)),
        out_shape=jax.ShapeDtypeStruct((m, n), F32),
        compiler_params=_cp("parallel"),
        name="mem_kv_norm",
    )(kv, g_k.reshape(1, hd))


def _mem_attend_kernel(q_ref, kv_ref, g_ref, o_ref, *, heads, hd):
    w = heads * hd
    for h in range(heads):
        q = q_ref[:, h * hd:(h + 1) * hd]
        qn = q * lax.rsqrt(jnp.mean(q * q, axis=-1, keepdims=True) + RMS_EPS) * g_ref[...]
        k = kv_ref[0, :, h * hd:(h + 1) * hd]
        v = kv_ref[0, :, w + h * hd:w + (h + 1) * hd]
        s = _dot(qn, k, NT) * (hd ** -0.5)
        p = jnp.exp(s - jnp.max(s, axis=-1, keepdims=True))
        o = _dot(p, v) / jnp.sum(p, axis=-1, keepdims=True)
        o_ref[:, h * hd:(h + 1) * hd] = o.astype(o_ref.dtype)


def _mem_attend(u, q_col, kv, g_q, *, bsz, t, tq, out_dtype):
    n_mem, w2 = kv.shape[1:]
    w = w2 // 2
    hd = w // MEM_HEADS
    tpb = t // tq
    return pl.pallas_call(
        functools.partial(_mem_attend_kernel, heads=MEM_HEADS, hd=hd),
        grid=(bsz * t // tq,),
        in_specs=[pl.BlockSpec((tq, w), lambda i: (i, q_col)),
                  pl.BlockSpec((1, n_mem, w2), lambda i: (i // tpb, 0, 0)),
                  pl.BlockSpec((1, hd), lambda i: (0, 0))],
        out_specs=pl.BlockSpec((tq, w), lambda i: (i, 0)),
        out_shape=jax.ShapeDtypeStruct((bsz * t, w), out_dtype),
        compiler_params=_cp("parallel"),
        name="mem_attend",
    )(u, kv, g_q.reshape(1, hd))


def _cumsum(x, tril_incl_bf16):
    hi = x.astype(BF16)
    lo = (x - hi.astype(F32)).astype(BF16)
    return (lax.dot_general(tril_incl_bf16, hi, NN, preferred_element_type=F32)
            + lax.dot_general(tril_incl_bf16, lo, NN, preferred_element_type=F32))


def _rwkv_kernel(ur, uk, uv, ul, zr, zk, zv, zl, mur, muk, muv, mul, w0, a0, kkp, kap, rkp, lg, lb, w2, a2, g2, s0_ref,
                 o_ref, st_ref, s_scr, pr_scr, pk_scr, pv_scr, pl_scr, *, hp, t_valid):
    c = pl.program_id(2)
    n_chunks = pl.num_programs(2)
    ln = ur.shape[0]
    l2 = 2 * ln
    hd = A_HD
    pairs = range(hp)

    @pl.when(c == 0)
    def _():
        s_scr[...] = s0_ref[0]
        pr_scr[...] = zr[0]
        pk_scr[...] = zk[0]
        pv_scr[...] = zv[0]
        pl_scr[...] = zl[0]

    def shift(x_ref, p_scr, mu_ref):
        x = x_ref[...]
        xp = pltpu.roll(x, 1, axis=0)
        row = lax.broadcasted_iota(jnp.int32, x.shape, 0)
        xp = jnp.where(row == 0, p_scr[...], xp)
        p_scr[...] = x[ln - 1:ln, :]
        return x + (xp - x) * mu_ref[...]

    r = shift(ur, pr_scr, mur)
    k = shift(uk, pk_scr, muk)
    v = shift(uv, pv_scr, muv)
    lo = shift(ul, pl_scr, mul)
    wl = lo[:, 0:LANES]
    al = lo[:, LANES:2 * LANES]
    gl = lo[:, 2 * LANES:2 * LANES + GATE_LORA]

    z = -(w0[...] + _dot(jnp.tanh(wl), w2[...]))
    lw = -jnp.exp(-(jnp.maximum(z, 0.0) + jnp.log(1.0 + jnp.exp(-jnp.abs(z)))) - 0.5)
    asig = _sigmoid(a0[...] + _dot(al, a2[...]))
    gate = _dot(_sigmoid(gl), g2[...])
    kk = k * kkp[...]
    k = k * (1.0 + (asig - 1.0) * kap[...])
    if t_valid < ln:
        valid = lax.broadcasted_iota(jnp.int32, (ln, 1), 0) < t_valid
        lw = jnp.where(valid, lw, 0.0)
        kk = jnp.where(valid, kk, 0.0)
        k = jnp.where(valid, k, 0.0)
        v = jnp.where(valid, v, 0.0)

    lane = lax.broadcasted_iota(jnp.int32, (1, LANES), 1)
    m0 = (lane < hd).astype(F32)
    m1 = 1.0 - m0
    row = lax.broadcasted_iota(jnp.int32, (ln, ln), 0)
    col = lax.broadcasted_iota(jnp.int32, (ln, ln), 1)
    tril_incl = (col <= row).astype(BF16)
    row2 = lax.broadcasted_iota(jnp.int32, (l2, l2), 0)
    col2 = lax.broadcasted_iota(jnp.int32, (l2, l2), 1)
    same = (row2 // ln) == (col2 // ln)
    strict2 = same & (col2 < row2)
    incl2 = same & (col2 <= row2)
    eye2 = (row2 == col2).astype(F32)
    rowk = lax.broadcasted_iota(jnp.int32, (LANES, LANES), 0)
    colk = lax.broadcasted_iota(jnp.int32, (LANES, LANES), 1)
    eyek = (rowk == colk).astype(F32)

    def sl(p):
        return slice(p * LANES, (p + 1) * LANES)

    def hsum(x):
        return (jnp.sum(x * m0, axis=-1, keepdims=True) * m0 + jnp.sum(x * m1, axis=-1, keepdims=True) * m1)

    def stack(x):
        return jnp.concatenate([x * m0, x * m1], axis=0)

    def unstack(xs):
        return xs[0:ln] + xs[ln:l2]

    r_ = [r[:, sl(p)] for p in pairs]
    k_ = [k[:, sl(p)] for p in pairs]
    v_ = [v[:, sl(p)] for p in pairs]
    lw_ = [lw[:, sl(p)] for p in pairs]
    kk_ = [kk[:, sl(p)] for p in pairs]
    kkn = [x / jnp.maximum(jnp.sqrt(hsum(x * x)), 1e-12) for x in kk_]
    a_ = [-x for x in kkn]
    b_ = [kkn[p] * asig[:, sl(p)] for p in pairs]

    cin_all = _cumsum(lw, tril_incl)
    cin = [cin_all[:, sl(p)] for p in pairs]
    cl = [x[ln - 1:ln, :] for x in cin]
    inv = [jnp.exp(-x) for x in cin]
    dec = [jnp.exp(cl[p] - cin[p]) for p in pairs]
    a_s = [stack(a_[p] * jnp.exp(cin[p] - lw_[p])) for p in pairs]
    r_s = [stack(r_[p] * jnp.exp(cin[p])) for p in pairs]
    b_s = [stack(b_[p] * inv[p]) for p in pairs]
    k_s = [stack(k_[p] * inv[p]) for p in pairs]
    v_s = [stack(x) for x in v_]
    bh_s = [stack(b_[p] * dec[p]) for p in pairs]
    kh_s = [stack(k_[p] * dec[p]) for p in pairs]

    g = [_dot(jnp.concatenate([a_s[p], r_s[p]], axis=0), jnp.concatenate([b_s[p], k_s[p]], axis=0), NT) for p in pairs]
    a_ab = [jnp.where(strict2, x[0:l2, 0:l2], 0.0) for x in g]
    a_ak = [jnp.where(strict2, x[0:l2, l2:2 * l2], 0.0) for x in g]
    r_b = [jnp.where(incl2, x[l2:2 * l2, 0:l2], 0.0) for x in g]
    r_k = [jnp.where(incl2, x[l2:2 * l2, l2:2 * l2], 0.0) for x in g]

    pw = a_ab
    tinv = [eye2 + x for x in a_ab]
    for _ in range(ln.bit_length() - 2):
        pw = [_dot(x, x) for x in pw]
        tinv = [tinv[p] + _dot(tinv[p], pw[p]) for p in pairs]

    av_s = [_dot(a_ak[p], v_s[p]) for p in pairs]
    wz = [_dot(tinv[p], jnp.concatenate([a_s[p], av_s[p]], axis=1)) for p in pairs]
    w_s = [x[:, 0:LANES] for x in wz]
    z_s = [x[:, LANES:2 * LANES] for x in wz]
    zv_s = [jnp.concatenate([z_s[p], v_s[p]], axis=0) for p in pairs]
    q = [unstack(r_s[p] + _dot(r_b[p], w_s[p])) for p in pairs]
    y_loc = [unstack(_dot(jnp.concatenate([r_b[p], r_k[p]], axis=1), zv_s[p])) for p in pairs]
    mc = [eyek * jnp.exp(cl[p]) + _dot(w_s[p], bh_s[p], TN) for p in pairs]
    nc = [_dot(zv_s[p], jnp.concatenate([bh_s[p], kh_s[p]], axis=0), TN) for p in pairs]

    s = [s_scr[p] for p in pairs]
    y = [_dot(q[p], s[p], NT) + y_loc[p] for p in pairs]
    s_new = [_dot(s[p], mc[p]) + nc[p] for p in pairs]
    for p in pairs:
        s_scr[p] = s_new[p]

    inv_hd = 1.0 / hd
    mean = [hsum(x) * inv_hd for x in y]
    d = [y[p] - mean[p] for p in pairs]
    var = [hsum(x * x) * inv_hd for x in d]
    bonus = [hsum(r_[p] * k_[p] * rkp[:, sl(p)]) * v_[p] for p in pairs]
    for p in pairs:
        yn = d[p] * lax.rsqrt(var[p] + LNX_EPS) * lg[:, sl(p)] + lb[:, sl(p)]
        o_ref[:, sl(p)] = ((yn + bonus[p]) * gate[:, sl(p)]).astype(o_ref.dtype)

    @pl.when(c == n_chunks - 1)
    def _():
        st_ref[0] = s_scr[...]


def _lora_slot(x, axis=-1):
    axis = axis % x.ndim
    parts = jnp.split(x, [DECAY_LORA, DECAY_LORA + AAA_LORA], axis=axis)
    sizes = (LANES, LANES, LORA_PAD - 2 * LANES)
    pad = lambda p, n: jnp.pad(p, [(0, n - p.shape[axis]) if ax == axis else (0, 0) for ax in range(x.ndim)])
    return jnp.concatenate([pad(p, n) for p, n in zip(parts, sizes)], axis=axis)


def _lora_unslot(x):
    return jnp.concatenate([x[..., :DECAY_LORA], x[..., LANES:LANES + AAA_LORA],
                            x[..., 2 * LANES:2 * LANES + GATE_LORA]], axis=-1)


def _rwkv_mixer(u_main, u_tail, prev_main, prev_lora, s0, prm, *, bsz, t, t_valid, hp):
    mu_main, mu_lora, w0, a0, kk, ka, rk, lnx_g, lnx_b, w2p, a2p, g2 = prm
    c = w0.shape[0]
    heads = c // A_HD
    n_pairs = heads // 2
    gw = hp * LANES
    ng = c // gw
    nch = t // CHUNK
    s0p = s0.reshape(bsz, n_pairs, 2, A_HD, A_HD)
    zero = jnp.zeros_like(s0p[:, :, 0])
    s0_bd = jnp.concatenate([jnp.concatenate([s0p[:, :, 0], zero], axis=-1),
                             jnp.concatenate([zero, s0p[:, :, 1]], axis=-1)], axis=-2)
    row1 = lambda a: a.reshape(1, -1)
    ublk = lambda off: pl.BlockSpec((CHUNK, gw), lambda bi, gi, ci: (bi * nch + ci, off * ng + gi))
    zblk = lambda off: pl.BlockSpec((1, 1, gw), lambda bi, gi, ci: (bi, 0, off * ng + gi))
    pblk = lambda off: pl.BlockSpec((1, gw), lambda bi, gi, ci: (0, off * ng + gi))
    lora_c = pl.BlockSpec((1, LORA_PAD), lambda bi, gi, ci: (0, 0))
    st_blk = pl.BlockSpec((1, hp, LANES, LANES), lambda bi, gi, ci: (bi, gi, 0, 0))
    in_specs = [
        ublk(0), ublk(1), ublk(2),
        pl.BlockSpec((CHUNK, LORA_PAD), lambda bi, gi, ci: (bi * nch + ci, 0)),
        zblk(0), zblk(1), zblk(2),
        pl.BlockSpec((1, 1, LORA_PAD), lambda bi, gi, ci: (bi, 0, 0)),
        pblk(0), pblk(1), pblk(2), lora_c,
        pblk(0), pblk(0), pblk(0), pblk(0), pblk(0), pblk(0), pblk(0),
        pl.BlockSpec((LANES, gw), lambda bi, gi, ci: (0, gi)),
        pl.BlockSpec((LANES, gw), lambda bi, gi, ci: (0, gi)),
        pl.BlockSpec((GATE_LORA, gw), lambda bi, gi, ci: (0, gi)),
        st_blk,
    ]
    out, st_bd = pl.pallas_call(
        functools.partial(_rwkv_kernel, hp=hp, t_valid=t_valid),
        grid=(bsz, ng, nch),
        in_specs=in_specs,
        out_specs=[pl.BlockSpec((CHUNK, gw), lambda bi, gi, ci: (bi * nch + ci, gi)), st_blk],
        out_shape=[jax.ShapeDtypeStruct((bsz * t, c), BF16),
                   jax.ShapeDtypeStruct((bsz, n_pairs, LANES, LANES), F32)],
        scratch_shapes=[pltpu.VMEM((hp, LANES, LANES), F32), pltpu.VMEM((1, gw), F32), pltpu.VMEM((1, gw), F32),
                        pltpu.VMEM((1, gw), F32), pltpu.VMEM((1, LORA_PAD), F32)],
        compiler_params=_cp("parallel", "parallel", "arbitrary"),
        name="rwkv_mixer",
    )(u_main, u_main, u_main, u_tail, prev_main, prev_main, prev_main, prev_lora,
      row1(mu_main), row1(mu_main), row1(mu_main), row1(mu_lora),
      row1(w0), row1(a0), row1(kk), row1(ka), row1(rk), row1(lnx_g), row1(lnx_b), w2p, a2p, g2, s0_bd)
    st = jnp.stack([st_bd[:, :, :A_HD, :A_HD], st_bd[:, :, A_HD:, A_HD:]], axis=2)
    return out, st.reshape(bsz, heads, A_HD, A_HD)


def _qk_norm(x, g):
    return x * lax.rsqrt(jnp.mean(x * x, axis=-1, keepdims=True) + RMS_EPS) * g


DSWA_NB = 4


def _dswa_prompt_kernel(q0, k0, v0, q1, k1, v1, q2, k2, v2, gq_ref, gk_ref,
                        o_ref, c0, c1, c2, qn_scr, kn_scr, og_scr, lse_scr):
    hd = q0.shape[1]
    t = q0.shape[0]
    scale = hd ** -0.5
    neg = -1e30
    groups = ((q0, k0, v0, c0), (q1, k1, v1, c1), (q2, k2, v2, c2))
    for g, ((q_ref, k_ref, v_ref, c_ref), (win, dil)) in enumerate(zip(groups, B_GROUPS)):
        span = win // dil
        nb = (t // dil) // span
        keep = c_ref.shape[1] // 2
        qn_scr[...] = _qk_norm(q_ref[...], gq_ref[...])
        kn_scr[...] = _qk_norm(k_ref[...], gk_ref[...])
        c_ref[0, pl.ds(0, keep, stride=2), :] = kn_scr[t - keep:t, :]
        c_ref[0, pl.ds(1, keep, stride=2), :] = v_ref[t - keep:t, :]
        row = lax.broadcasted_iota(jnp.int32, (span, span), 0)
        col = lax.broadcasted_iota(jnp.int32, (span, span), 1)
        own_ok = col <= row
        prev_ok = col >= row
        per_it = min(DSWA_NB, nb)
        n_res = DSWA_NB // per_it
        its_per_res = nb // per_it

        def rows_at(start, dil=dil, span=span):
            return pl.ds(start, span) if dil == 1 else pl.ds(start, span, stride=dil)

        def body(it, carry, dil=dil, span=span, g=g, v_ref=v_ref, per_it=per_it, n_res=n_res,
                 its_per_res=its_per_res, own_ok=own_ok, prev_ok=prev_ok, rows_at=rows_at):
            blocks = []
            for rr in range(n_res):
                res = (it // its_per_res) * n_res + rr
                blk0 = (it % its_per_res) * per_it
                for jb in range(per_it):
                    start = res + dil * span * (blk0 + jb)
                    if jb > 0:
                        blocks.append((start, "shared", None))
                    elif its_per_res > 1:
                        blocks.append((start, jnp.maximum(start - dil * span, res), blk0 > 0))
                    else:
                        blocks.append((start, None, None))
            n = len(blocks)
            cur = [rows_at(bk[0]) for bk in blocks]
            qb = [qn_scr[c, :].astype(BF16) for c in cur]
            kc = [kn_scr[c, :].astype(BF16) for c in cur]
            vc = [v_ref[c, :].astype(BF16) for c in cur]
            kp, vp, pmask = [], [], []
            for j, (start, prv, has_prev) in enumerate(blocks):
                if prv is None:
                    kp.append(None)
                    vp.append(None)
                    pmask.append(None)
                elif isinstance(prv, str):
                    kp.append(kc[j - 1])
                    vp.append(vc[j - 1])
                    pmask.append(prev_ok)
                else:
                    rws = rows_at(prv)
                    kp.append(kn_scr[rws, :].astype(BF16))
                    vp.append(v_ref[rws, :].astype(BF16))
                    pmask.append(prev_ok & has_prev)
            s_c = [jnp.where(own_ok, _dot(qb[j], kc[j], NT) * scale, neg) for j in range(n)]
            s_p = [None if kp[j] is None else jnp.where(pmask[j], _dot(qb[j], kp[j], NT) * scale, neg)
                   for j in range(n)]
            m = [jnp.max(s_c[j], axis=-1, keepdims=True) if s_p[j] is None else
                 jnp.maximum(jnp.max(s_c[j], axis=-1, keepdims=True), jnp.max(s_p[j], axis=-1, keepdims=True))
                 for j in range(n)]
            p_c = [jnp.exp(s_c[j] - m[j]) for j in range(n)]
            p_p = [None if s_p[j] is None else jnp.exp(s_p[j] - m[j]) for j in range(n)]
            l = [jnp.sum(p_c[j], axis=-1, keepdims=True) if p_p[j] is None else
                 jnp.sum(p_c[j], axis=-1, keepdims=True) + jnp.sum(p_p[j], axis=-1, keepdims=True)
                 for j in range(n)]
            o = [_dot(p_c[j], vc[j]) if p_p[j] is None else _dot(p_c[j], vc[j]) + _dot(p_p[j], vp[j])
                 for j in range(n)]
            for j in range(n):
                og_scr[g, cur[j], :] = o[j] / l[j]
                lse_scr[g, cur[j], :] = jnp.broadcast_to(m[j] + jnp.log(l[j]), (span, hd))
            return carry

        lax.fori_loop(0, (dil * nb) // DSWA_NB, body, 0)

    l0, l1, l2 = lse_scr[0], lse_scr[1], lse_scr[2]
    m = jnp.maximum(jnp.maximum(l0, l1), l2)
    e0, e1, e2 = jnp.exp(l0 - m), jnp.exp(l1 - m), jnp.exp(l2 - m)
    o = (e0 * og_scr[0] + e1 * og_scr[1] + e2 * og_scr[2]) / (e0 + e1 + e2)
    o_ref[...] = o.astype(o_ref.dtype)


def _dswa_prompt(u, g_q, g_k, *, bsz, t, gh):
    hd = B_HD
    ng = len(B_GROUPS)
    nh = ng * gh
    in_specs = []
    for g in range(ng):
        for part in range(3):
            in_specs.append(pl.BlockSpec((t, hd), lambda bi, h, g=g, part=part: (bi, part * nh + g * gh + h)))
    in_specs += [pl.BlockSpec((1, hd), lambda bi, h: (0, 0))] * 2
    keeps = [min(win, t) for win, _ in B_GROUPS]
    res = pl.pallas_call(
        _dswa_prompt_kernel,
        grid=(bsz, gh),
        in_specs=in_specs,
        out_specs=[pl.BlockSpec((t, hd), lambda bi, h: (bi, h))]
        + [pl.BlockSpec((1, 2 * kp, hd), lambda bi, h: (bi, 0, h)) for kp in keeps],
        out_shape=[jax.ShapeDtypeStruct((bsz * t, gh * hd), BF16)]
        + [jax.ShapeDtypeStruct((bsz, 2 * kp, gh * hd), F32) for kp in keeps],
        scratch_shapes=[pltpu.VMEM((t, hd), F32), pltpu.VMEM((t, hd), F32),
                        pltpu.VMEM((ng, t, hd), F32), pltpu.VMEM((ng, t, hd), F32)],
        compiler_params=_cp("parallel", "parallel"),
        name="dswa_prompt",
    )(*([u] * 9), g_q.reshape(1, hd), g_k.reshape(1, hd))
    return res[0], [b.reshape(bsz, kp, 2, gh, hd) for b, kp in zip(res[1:], keeps)]


def _dswa_sample_group_kernel(q_ref, k_ref, v_ref, c_ref, nx_ref, gq_ref, gk_ref, o_ref, lse_ref, n_ref,
                              m_scr, l_scr, a_scr, *, win, dil, rows, tr, gh):
    i = pl.program_id(1)
    n_t = pl.num_programs(1)
    ts = q_ref.shape[0]
    hd = B_HD
    rpp = 2 * gh
    frows = tr * rpp
    new_rows = ts * rpp
    scale = hd ** -0.5
    neg = -1e30
    heads = range(gh)
    hs = lambda h: slice(h * hd, (h + 1) * hd)
    rs = lambda h: slice(h * ts, (h + 1) * ts)

    @pl.when(i == 0)
    def _():
        m_scr[...] = jnp.full(m_scr.shape, neg, F32)
        l_scr[...] = jnp.zeros(l_scr.shape, F32)
        a_scr[...] = jnp.zeros(a_scr.shape, F32)

    qn = [_qk_norm(q_ref[:, hs(h)], gq_ref[...]) for h in heads]
    kc = [c_ref[0, pl.ds(h, tr, stride=rpp), :] for h in heads]
    vc = [c_ref[0, pl.ds(gh + h, tr, stride=rpp), :] for h in heads]
    dist = (rows + lax.broadcasted_iota(jnp.int32, (ts, tr), 0)
            - (i * tr + lax.broadcasted_iota(jnp.int32, (ts, tr), 1)))
    ok = (dist <= win) & ((dist % dil) == 0)

    def fold(s, okm, v, h):
        s = jnp.where(okm, s, neg)
        m_old = m_scr[rs(h), :]
        m_new = jnp.maximum(m_old, jnp.max(s, axis=-1, keepdims=True))
        alpha = jnp.exp(m_old - m_new)
        p = jnp.where(okm, jnp.exp(s - m_new[:, 0:1]), 0.0)
        m_scr[rs(h), :] = m_new
        l_scr[rs(h), :] = alpha * l_scr[rs(h), :] + jnp.sum(p, axis=-1, keepdims=True)
        a_scr[rs(h), :] = alpha * a_scr[rs(h), :] + _dot(p, v)

    s_old = [_dot(qn[h], kc[h], NT) * scale for h in heads]
    for h in heads:
        fold(s_old[h], ok, vc[h], h)

    n_ref[0, 0:frows - new_rows, :] = c_ref[0, new_rows:frows, :]

    @pl.when(i < n_t - 1)
    def _():
        n_ref[0, frows - new_rows:frows, :] = nx_ref[0]

    @pl.when(i == n_t - 1)
    def _():
        d_n = lax.broadcasted_iota(jnp.int32, (ts, ts), 0) - lax.broadcasted_iota(jnp.int32, (ts, ts), 1)
        ok_n = (d_n >= 0) & ((d_n % dil) == 0)
        kn = [_qk_norm(k_ref[:, hs(h)], gk_ref[...]) for h in heads]
        vn = [v_ref[:, hs(h)] for h in heads]
        s_new = [_dot(qn[h], kn[h], NT) * scale for h in heads]
        for h in heads:
            fold(s_new[h], ok_n, vn[h], h)
            n_ref[0, pl.ds(frows - new_rows + h, ts, stride=rpp), :] = kn[h]
            n_ref[0, pl.ds(frows - new_rows + gh + h, ts, stride=rpp), :] = vn[h]
            o_ref[:, hs(h)] = a_scr[rs(h), :] / l_scr[rs(h), :]
            lse_ref[:, hs(h)] = m_scr[rs(h), :] + jnp.log(l_scr[rs(h), :])


def _dswa_merge_kernel(o0, l0, o1, l1, o2, l2, o_ref):
    m = jnp.maximum(jnp.maximum(l0[...], l1[...]), l2[...])
    e0, e1, e2 = jnp.exp(l0[...] - m), jnp.exp(l1[...] - m), jnp.exp(l2[...] - m)
    o_ref[...] = (e0 * o0[...] + e1 * o1[...] + e2 * o2[...]) / (e0 + e1 + e2)


def _dswa_sample(u, caches, g_q, g_k, *, bsz, ts):
    hd = B_HD
    ng = len(B_GROUPS)
    gh = caches[0].shape[3]
    gw = gh * hd
    rpp = 2 * gh
    new_rows = ts * rpp
    parts, bufs = [], []
    for g, (cch, (win, dil)) in enumerate(zip(caches, B_GROUPS)):
        rows = cch.shape[1]
        tr = min(rows, 512)
        n_t = rows // tr
        frows = tr * rpp
        flat = cch.reshape(bsz, rows * rpp, hd)
        nx_last = rows * rpp // new_rows - 1
        ublk = lambda part, g=g: pl.BlockSpec((ts, gw), lambda bi, i: (bi, part * ng + g))
        cblk = pl.BlockSpec((1, frows, hd), lambda bi, i: (bi, i, 0))
        nxblk = pl.BlockSpec((1, new_rows, hd),
                             lambda bi, i, frows=frows, nx_last=nx_last: (bi, jnp.minimum((i + 1) * (frows // new_rows), nx_last), 0))
        oblk = pl.BlockSpec((ts, gw), lambda bi, i: (bi, 0))
        o, lse, nbuf = pl.pallas_call(
            functools.partial(_dswa_sample_group_kernel, win=win, dil=dil, rows=rows, tr=tr, gh=gh),
            grid=(bsz, n_t),
            in_specs=[ublk(0), ublk(1), ublk(2), cblk, nxblk,
                      pl.BlockSpec((1, hd), lambda bi, i: (0, 0)), pl.BlockSpec((1, hd), lambda bi, i: (0, 0))],
            out_specs=[oblk, oblk, cblk],
            out_shape=[jax.ShapeDtypeStruct((bsz * ts, gw), F32), jax.ShapeDtypeStruct((bsz * ts, gw), F32),
                       jax.ShapeDtypeStruct(flat.shape, F32)],
            scratch_shapes=[pltpu.VMEM((gh * ts, hd), F32)] * 3,
            compiler_params=_cp("parallel", "arbitrary"),
            name="dswa_sample",
        )(u, u, u, flat, flat, g_q.reshape(1, hd), g_k.reshape(1, hd))
        parts += [o, lse]
        bufs.append(nbuf.reshape(cch.shape))
    blk = pl.BlockSpec((bsz * ts, gw), lambda: (0, 0))
    merged = pl.pallas_call(
        _dswa_merge_kernel,
        in_specs=[blk] * 6,
        out_specs=blk,
        out_shape=jax.ShapeDtypeStruct((bsz * ts, gw), F32),
        name="dswa_merge",
    )(*parts)
    return merged, bufs


def kernel(x_prompt, x_sample, state_wkv, state_shift, cache_swa_kv1, cache_swa_kv2, cache_swa_kv3, cache_mem_kv, mem_prompt, norm_mix, norm_ffn, norm_mem, w_mem_kv, q_norm_mem, k_norm_mem, w_in_a, w_out_a, mu_a, w0_a, w2_a, a0_a, a2_a, g2_a, kk_a, ka_a, rk_a, lnx_g_a, lnx_b_a, w_in_b, w_out_b, q_norm_b, k_norm_b, w_ffn_in, w_ffn_out):
    bp, t, d = x_prompt.shape
    bs, ts, _ = x_sample.shape
    n_mem = mem_prompt.shape[1]
    mem_w = w_mem_kv.shape[2] // 2
    mem_hd = mem_w // MEM_HEADS
    mix_w = w0_a.shape[1]
    depth = norm_mix.shape[0]
    swa_in = (cache_swa_kv1, cache_swa_kv2, cache_swa_kv3)
    mp_rows = bp * t
    ms_rows = bs * ts
    tm = 512

    yp = x_prompt.reshape(mp_rows, d)
    ys = x_sample.reshape(ms_rows, d)
    mem2d = mem_prompt.reshape(bp * n_mem, d)

    wkv_p, shift_p, wkv_s, shift_s, mem_p = [], [], [], [], []
    swa_p = [[] for _ in B_GROUPS]
    swa_s = [[] for _ in B_GROUPS]
    for i in range(depth):
        j = i // 2
        mkv_p = _mem_kv(mem2d, norm_mem[i], w_mem_kv, i, k_norm_mem[i], hd=mem_hd)
        mem_p.append(mkv_p.reshape(bp, n_mem, 2, MEM_HEADS, mem_hd))
        mkv_p = mkv_p.reshape(bp, n_mem, 2 * mem_w)
        mkv_s = cache_mem_kv[i].reshape(bs, n_mem, 2 * mem_w)
        hp_ = _rmsnorm(yp, norm_mix[i])
        hs_ = _rmsnorm(ys, norm_mix[i])
        if i % 2 == 0:
            rkv_w = 3 * mix_w
            a_shift_w = mu_a.shape[1]
            w_nk = jnp.swapaxes(w_in_a, 1, 2)
            w_tail = jnp.concatenate([_lora_slot(w_nk[j, rkv_w:a_shift_w], axis=0), w_nk[j, a_shift_w:]], axis=0)
            um_p, um_s = _mm_panel2(hp_, hs_, w_nk, j, n=rkv_w, tm=tm, spp=6, w_t=True, name="w_in_a")
            ut_p, ut_s = _mm_panel2(hp_, hs_, w_tail[None], 0, tm=tm, spp=4, w_t=True, name="w_in_a_tail")
            pad_rows = lambda w: jnp.pad(w, ((0, LANES - w.shape[0]), (0, 0)))
            prm = (mu_a[j][:rkv_w], _lora_slot(mu_a[j][rkv_w:]), w0_a[j], a0_a[j], kk_a[j], ka_a[j],
                   rk_a[j].reshape(-1), lnx_g_a[j], lnx_b_a[j], pad_rows(w2_a[j]), pad_rows(a2_a[j]), g2_a[j])
            heads = mix_w // A_HD
            mp, sp = _rwkv_mixer(um_p, ut_p, jnp.zeros((bp, 1, rkv_w), F32), jnp.zeros((bp, 1, LORA_PAD), F32),
                                 jnp.zeros((bp, heads, A_HD, A_HD), F32), prm, bsz=bp, t=t, t_valid=CHUNK, hp=24)
            padt = lambda z: jnp.pad(z.reshape(bs, ts, -1), ((0, 0), (0, CHUNK - ts), (0, 0))).reshape(bs * CHUNK, -1)
            ms, ss = _rwkv_mixer(padt(um_s), padt(ut_s[:, :LORA_PAD]), state_shift[j][:, :, :rkv_w],
                                 _lora_slot(state_shift[j][:, :, rkv_w:]), state_wkv[j], prm, bsz=bs, t=CHUNK,
                                 t_valid=ts, hp=24)
            ms = ms.reshape(bs, CHUNK, mix_w)[:, :ts].reshape(ms_rows, mix_w)
            ap = _mem_attend(ut_p, 1, mkv_p, q_norm_mem[i], bsz=bp, t=t, tq=512, out_dtype=BF16)
            as_ = _mem_attend(ut_s, 1, mkv_s, q_norm_mem[i], bsz=bs, t=ts, tq=ts, out_dtype=F32)
            wkv_p.append(sp)
            wkv_s.append(ss)
            last = lambda z, b_, t_: z.reshape(b_, t_, -1)[:, -1:]
            shift_p.append(jnp.concatenate([last(um_p, bp, t), _lora_unslot(last(ut_p[:, :LORA_PAD], bp, t))], axis=-1))
            shift_s.append(jnp.concatenate([last(um_s, bs, ts), _lora_unslot(last(ut_s[:, :LORA_PAD], bs, ts))], axis=-1))
            w_out = w_out_a
        else:
            u_p, u_s = _mm_panel2(hp_, hs_, w_in_b, j, tm=tm, spp=5, name="w_in_b")
            gh = cache_swa_kv1.shape[4]
            mp, bufs_p = _dswa_prompt(u_p, q_norm_b[j], k_norm_b[j], bsz=bp, t=t, gh=gh)
            ms, bufs_s = _dswa_sample(u_s, [c[j] for c in swa_in], q_norm_b[j], k_norm_b[j], bsz=bs, ts=ts)
            mem_col = (3 * mix_w) // mem_w
            ap = _mem_attend(u_p, mem_col, mkv_p, q_norm_mem[i], bsz=bp, t=t, tq=512, out_dtype=BF16)
            as_ = _mem_attend(u_s, mem_col, mkv_s, q_norm_mem[i], bsz=bs, t=ts, tq=ts, out_dtype=F32)
            for g in range(len(B_GROUPS)):
                swa_p[g].append(bufs_p[g])
                swa_s[g].append(bufs_s[g])
            w_out = w_out_b
        os_ = jnp.concatenate([ms.astype(BF16), as_.astype(BF16)], axis=1)
        yp, ys = _mm_panel((mp, ap), os_, w_out, j, tm=tm, spp=8, res=(yp, ys), name="w_out")
        yp, ys = _ffn(yp, ys, norm_ffn[i], w_ffn_in, w_ffn_out, i)
    return (yp.reshape(bp, t, d), ys.reshape(bs, ts, d),
            jnp.stack(wkv_p), jnp.stack(shift_p),
            jnp.stack(swa_p[0]), jnp.stack(swa_p[1]), jnp.stack(swa_p[2]),
            jnp.stack(mem_p),
            jnp.stack(wkv_s), jnp.stack(shift_s),
            jnp.stack(swa_s[0]), jnp.stack(swa_s[1]), jnp.stack(swa_s[2]))
```
